```python
import jax, jax.numpy as jnp
from jax import lax
import numpy as np

D_MODEL = 1024
BATCH = 32
SEQ = 2048
DEPTH = 2

N_MIXERS = 2
N_CONV_LAYERS = (DEPTH + 1) // 2
N_RET_LAYERS = DEPTH // 2
CONV_WIDTH = 3
RET_HEADS = 4
RET_QK_DIM = D_MODEL // RET_HEADS
RET_V_DIM = 2 * RET_QK_DIM
RET_QK_WIDTH = RET_HEADS * RET_QK_DIM
RET_V_WIDTH = RET_HEADS * RET_V_DIM
RET_CHUNK = 128
ROPE_BASE = 10000.0
N_EXPERTS = 16
EC_CAPACITY_FACTOR = 2
EXPERT_FF = 2 * D_MODEL
PLE_DIM = 256
NORM_EPS = 1e-6
GN_EPS = 1e-5

kernel_name = "hybrid_conv_retention_ec_moe_encoder"


def rms_norm(x, g):
    xf = x.astype(jnp.float32)
    y = xf * lax.rsqrt(jnp.mean(xf * xf, axis=-1, keepdims=True) + NORM_EPS)
    return (y * g.astype(jnp.float32)).astype(x.dtype)


def short_conv_mixer(h, w_in, w_conv, b_conv, w_out):
    b_gate, c_gate, v = jnp.split(h @ w_in, 3, axis=-1)
    u = c_gate * v
    kern = w_conv[:, None, :].astype(u.dtype)
    pad = CONV_WIDTH // 2
    y = lax.conv_general_dilated(u, kern, window_strides=(1,), padding=((pad, pad),),
                                 dimension_numbers=('NWC', 'WIO', 'NWC'),
                                 feature_group_count=u.shape[-1]) + b_conv.astype(u.dtype)
    return (b_gate * y) @ w_out


def rotate(x, pos):
    half = x.shape[-1] // 2
    inv = ROPE_BASE ** (-jnp.arange(half, dtype=jnp.float32) / half)
    ang = pos.astype(jnp.float32)[:, None] * inv[None, :]
    cos = jnp.cos(ang)[:, None, :]
    sin = jnp.sin(ang)[:, None, :]
    x1, x2 = x[..., :half], x[..., half:]
    return jnp.concatenate([x1 * cos - x2 * sin, x1 * sin + x2 * cos], axis=-1)


def retention_one_direction(q, k, v, log_gamma, strict):
    b, h, s, dk = q.shape
    dv = v.shape[-1]
    L = RET_CHUNK
    n = s // L
    lg = log_gamma.astype(jnp.float32)
    idx = jnp.arange(L, dtype=jnp.float32)
    diff = idx[:, None] - idx[None, :]
    mask = (diff > 0) if strict else (diff >= 0)
    intra_decay = jnp.where(mask[None], jnp.exp(lg[:, None, None] * jnp.where(mask, diff, 0.0)[None]), 0.0)
    q_decay = jnp.exp(lg[:, None] * (idx[None, :] + 1.0))
    k_decay = jnp.exp(lg[:, None] * (L - 1.0 - idx[None, :]))
    chunk_decay = jnp.exp(lg * L)

    def to_chunks(t):
        return jnp.moveaxis(t.reshape(b, h, n, L, t.shape[-1]), 2, 0)

    def step(state, inp):
        qi, ki, vi = inp
        scores = jnp.einsum('bhid,bhjd->bhij', qi, ki) * intra_decay[None]
        intra = jnp.einsum('bhij,bhjv->bhiv', scores, vi)
        cross = jnp.einsum('bhld,bhdv->bhlv', qi * q_decay[None, :, :, None], state)
        state = state * chunk_decay[None, :, None, None] + jnp.einsum(
            'bhld,bhlv->bhdv', ki * k_decay[None, :, :, None], vi)
        return state, intra + cross

    init = jnp.zeros((b, h, dk, dv), jnp.float32)
    _, out = lax.scan(step, init, (to_chunks(q), to_chunks(k), to_chunks(v)))
    return jnp.moveaxis(out, 0, 2).reshape(b, h, s, dv)


def retention_mixer(h, w_in, log_decay, w_out):
    b, s, _ = h.shape
    proj = h @ w_in
    q, k, v, g = jnp.split(proj, [RET_QK_WIDTH, 2 * RET_QK_WIDTH, 2 * RET_QK_WIDTH + RET_V_WIDTH], axis=-1)
    pos = jnp.arange(s)
    q = rotate(q.reshape(b, s, RET_HEADS, RET_QK_DIM).astype(jnp.float32), pos)
    k = rotate(k.reshape(b, s, RET_HEADS, RET_QK_DIM).astype(jnp.float32), pos) * (RET_QK_DIM ** -0.5)
    v = v.reshape(b, s, RET_HEADS, RET_V_DIM).astype(jnp.float32)
    q, k, v = (jnp.transpose(t, (0, 2, 1, 3)) for t in (q, k, v))
    fwd = retention_one_direction(q, k, v, log_decay[0], strict=False)
    bwd = jnp.flip(retention_one_direction(jnp.flip(q, 2), jnp.flip(k, 2), jnp.flip(v, 2),
                                           log_decay[1], strict=True), 2)
    o = fwd + bwd
    mu = jnp.mean(o, axis=-1, keepdims=True)
    var = jnp.mean(jnp.square(o - mu), axis=-1, keepdims=True)
    o = (o - mu) * lax.rsqrt(var + GN_EPS)
    o = jnp.transpose(o, (0, 2, 1, 3)).reshape(b, s, RET_V_WIDTH).astype(h.dtype)
    return (jax.nn.silu(g) * o) @ w_out


def expert_choice_ffn(h, router_w, w_gate, w_up, w_down):
    b, s, _ = h.shape
    cap = EC_CAPACITY_FACTOR * s // N_EXPERTS
    logits = jnp.einsum('bsd,de->bse', h.astype(jnp.float32), router_w.astype(jnp.float32))
    affinity = jax.nn.softmax(logits, axis=-1)
    gates, tok = lax.top_k(jnp.swapaxes(affinity, 1, 2), cap)
    tok_e = jnp.swapaxes(tok, 0, 1)
    gates_e = jnp.swapaxes(gates, 0, 1)
    bidx = jnp.arange(b)[:, None]

    def run_expert(args):
        wg, wu, wd, t = args
        xe = h[bidx, t]
        return (jax.nn.silu(xe @ wg) * (xe @ wu)) @ wd

    y = lax.map(run_expert, (w_gate, w_up, w_down, tok_e))
    y = y * gates_e[..., None].astype(y.dtype)
    return jnp.zeros_like(h).at[jnp.arange(b)[None, :, None], tok_e].add(y)


def setup_inputs(seed: int = 0) -> dict:
    key = jax.random.key(seed)
    ks = jax.random.split(key, 20)
    f32 = jnp.float32
    D, E, F = D_MODEL, N_EXPERTS, EXPERT_FF

    def w(k, shape, fan_in):
        return jax.random.normal(k, shape, f32) * (fan_in ** -0.5)

    base_decay = jnp.log(1.0 - 2.0 ** (-5.0 - jnp.arange(RET_HEADS, dtype=f32)))
    ret_log_decay = base_decay[None, None, :] * (1.0 + 0.05 * jax.random.normal(ks[10], (N_RET_LAYERS, 2, RET_HEADS), f32))
    return {
        "x": jax.random.normal(ks[0], (BATCH, SEQ, D), f32),
        "p": jax.random.normal(ks[1], (DEPTH, BATCH, SEQ, PLE_DIM), f32),
        "norm_mix": 1.0 + 0.01 * jax.random.normal(ks[2], (DEPTH, D), f32),
        "norm_ffn": 1.0 + 0.01 * jax.random.normal(ks[3], (DEPTH, D), f32),
        "norm_ple": 1.0 + 0.01 * jax.random.normal(ks[4], (DEPTH, D), f32),
        "final_norm": 1.0 + 0.01 * jax.random.normal(ks[5], (D,), f32),
        "conv_w_in": w(ks[6], (N_CONV_LAYERS, D, 3 * D), D),
        "conv_w": w(ks[7], (N_CONV_LAYERS, CONV_WIDTH, D), CONV_WIDTH),
        "conv_b": 0.01 * jax.random.normal(ks[8], (N_CONV_LAYERS, D), f32),
        "conv_w_out": w(ks[9], (N_CONV_LAYERS, D, D), D),
        "ret_w_in": w(ks[11], (N_RET_LAYERS, D, 2 * RET_QK_WIDTH + 2 * RET_V_WIDTH), D),
        "ret_log_decay": ret_log_decay,
        "ret_w_out": w(ks[12], (N_RET_LAYERS, RET_V_WIDTH, D), RET_V_WIDTH),
        "router_w": w(ks[13], (DEPTH, D, E), D),
        "exp_w_gate": w(ks[14], (DEPTH, E, D, F), D),
        "exp_w_up": w(ks[15], (DEPTH, E, D, F), D),
        "exp_w_down": w(ks[16], (DEPTH, E, F, D), F),
        "ple_w_proj": w(ks[17], (DEPTH, PLE_DIM, D), PLE_DIM),
        "ple_w_gate": w(ks[18], (DEPTH, D, D), D),
    }


def reference(x, p, norm_mix, norm_ffn, norm_ple, final_norm, conv_w_in, conv_w, conv_b, conv_w_out,
              ret_w_in, ret_log_decay, ret_w_out, router_w, exp_w_gate, exp_w_up, exp_w_down,
              ple_w_proj, ple_w_gate):
    h = x
    for i in range(DEPTH):
        j = i // N_MIXERS
        hn = rms_norm(h, norm_mix[i])
        if i % N_MIXERS == 0:
            mix = short_conv_mixer(hn, conv_w_in[j], conv_w[j], conv_b[j], conv_w_out[j])
        else:
            mix = retention_mixer(hn, ret_w_in[j], ret_log_decay[j], ret_w_out[j])
        h = h + mix
        h = h + expert_choice_ffn(rms_norm(h, norm_ffn[i]), router_w[i], exp_w_gate[i], exp_w_up[i], exp_w_down[i])
        gate = jax.nn.sigmoid(rms_norm(h, norm_ple[i]) @ ple_w_gate[i])
        h = h + gate * (p[i].astype(h.dtype) @ ple_w_proj[i])
    return rms_norm(h, final_norm)
```

```python
import functools

import jax
import jax.numpy as jnp
from jax import lax
from jax.experimental import pallas as pl
from jax.experimental.pallas import tpu as pltpu

F32 = jnp.float32
BF16 = jnp.bfloat16
I32 = jnp.int32

NORM_EPS = 1e-6
GN_EPS = 1e-5
ROPE_BASE = 10000.0
EC_CAPACITY_FACTOR = 2
RET_CHUNK = 256
LANES = 128
SUBLANES = 8
VMEM_LIMIT = 56 * 1024 * 1024


def _cparams(*sem):
    return pltpu.CompilerParams(dimension_semantics=sem, vmem_limit_bytes=VMEM_LIMIT)


def _row_tile(n, target):
    t = min(n, target)
    while n % t:
        t //= 2
    return t


def _rms(x, g):
    return x * lax.rsqrt(jnp.mean(x * x, axis=-1, keepdims=True) + NORM_EPS) * g


def _sigmoid(x):
    return 1.0 / (1.0 + jnp.exp(-x))


def _dot(a, b):
    return jnp.dot(a, b, preferred_element_type=F32)


def _conv_in_kernel(x_ref, g_ref, w_ref, u_ref, bg_ref):
    d = x_ref.shape[1]
    hn = _rms(x_ref[...], g_ref[...]).astype(BF16)
    bg_ref[...] = _dot(hn, w_ref[:, :d]).astype(BF16)
    u_ref[...] = _dot(hn, w_ref[:, d:2 * d]) * _dot(hn, w_ref[:, 2 * d:])


def _conv_in(x2d, g, w_in):
    t, d = x2d.shape
    tm = _row_tile(t, 512)
    return pl.pallas_call(
        _conv_in_kernel,
        grid=(t // tm,),
        in_specs=[
            pl.BlockSpec((tm, d), lambda i: (i, 0)),
            pl.BlockSpec((1, d), lambda i: (0, 0)),
            pl.BlockSpec((d, 3 * d), lambda i: (0, 0)),
        ],
        out_specs=[
            pl.BlockSpec((tm, d), lambda i: (i, 0)),
            pl.BlockSpec((tm, d), lambda i: (i, 0)),
        ],
        out_shape=[
            jax.ShapeDtypeStruct((t, d), F32),
            jax.ShapeDtypeStruct((t, d), BF16),
        ],
        compiler_params=_cparams("parallel"),
        name="conv_in",
    )(x2d, g, w_in)


def _conv_out_kernel(u_ref, up_ref, un_ref, bg_ref, x_ref, cw_ref, cb_ref, w_ref, o_ref):
    j = pl.program_id(1)
    nj = pl.num_programs(1)
    u = u_ref[0]
    ts = u.shape[0]
    prev_row = jnp.where(j > 0, up_ref[0][SUBLANES - 1:SUBLANES, :], 0.0)
    next_row = jnp.where(j < nj - 1, un_ref[0][0:1, :], 0.0)
    row = lax.broadcasted_iota(I32, u.shape, 0)
    u_prev = jnp.where(row == 0, prev_row, pltpu.roll(u, 1, axis=0))
    u_next = jnp.where(row == ts - 1, next_row, pltpu.roll(u, ts - 1, axis=0))
    cw = cw_ref[...]
    y = cw[0:1, :] * u_prev + cw[1:2, :] * u + cw[2:3, :] * u_next + cb_ref[...]
    z = (bg_ref[0].astype(F32) * y).astype(BF16)
    o_ref[0] = x_ref[0] + _dot(z, w_ref[...])


def _conv_out(u, bg, x, conv_w, conv_b, w_out):
    b, s, d = x.shape
    ts = _row_tile(s, 512)
    nh = ts // SUBLANES
    last = s // SUBLANES - 1
    return pl.pallas_call(
        _conv_out_kernel,
        grid=(b, s // ts),
        in_specs=[
            pl.BlockSpec((1, ts, d), lambda i, j: (i, j, 0)),
            pl.BlockSpec((1, SUBLANES, d), lambda i, j: (i, jnp.maximum(j * nh - 1, 0), 0)),
            pl.BlockSpec((1, SUBLANES, d), lambda i, j: (i, jnp.minimum((j + 1) * nh, last), 0)),
            pl.BlockSpec((1, ts, d), lambda i, j: (i, j, 0)),
            pl.BlockSpec((1, ts, d), lambda i, j: (i, j, 0)),
            pl.BlockSpec((3, d), lambda i, j: (0, 0)),
            pl.BlockSpec((1, d), lambda i, j: (0, 0)),
            pl.BlockSpec((d, d), lambda i, j: (0, 0)),
        ],
        out_specs=pl.BlockSpec((1, ts, d), lambda i, j: (i, j, 0)),
        out_shape=jax.ShapeDtypeStruct((b, s, d), F32),
        compiler_params=_cparams("parallel", "parallel"),
        name="conv_out",
    )(u, u, u, bg, x, conv_w, conv_b, w_out)


def _ret_in_kernel(heads, x_ref, g_ref, wq_ref, wkt_ref, wv_ref, wg_ref,
                   cos_ref, sin_ref, cost_ref, sint_ref,
                   q_ref, kt_ref, v_ref, gate_ref):
    hn = _rms(x_ref[0], g_ref[...]).astype(BF16)
    hq = wq_ref.shape[1]
    dk = hq // heads
    half = dk // 2
    q = _dot(hn, wq_ref[...])
    cos = cos_ref[...]
    sin = sin_ref[...]
    for h in range(heads):
        x1 = q[:, h * dk:h * dk + half]
        x2 = q[:, h * dk + half:(h + 1) * dk]
        q_ref[0, :, h * dk:h * dk + half] = (x1 * cos - x2 * sin).astype(BF16)
        q_ref[0, :, h * dk + half:(h + 1) * dk] = (x1 * sin + x2 * cos).astype(BF16)
    kt = lax.dot_general(wkt_ref[...], hn, (((1,), (1,)), ((), ())),
                         preferred_element_type=F32)
    cost = cost_ref[...]
    sint = sint_ref[...]
    scale = dk ** -0.5
    for h in range(heads):
        x1 = kt[h * dk:h * dk + half, :]
        x2 = kt[h * dk + half:(h + 1) * dk, :]
        kt_ref[0, h * dk:h * dk + half, :] = ((x1 * cost - x2 * sint) * scale).astype(BF16)
        kt_ref[0, h * dk + half:(h + 1) * dk, :] = ((x1 * sint + x2 * cost) * scale).astype(BF16)
    v_ref[0] = _dot(hn, wv_ref[...]).astype(BF16)
    gate_ref[0] = _dot(hn, wg_ref[...]).astype(BF16)


def _ret_in(x, g, wq, wkt, wv, wg, cos, sin, heads):
    b, s, d = x.shape
    hq = wq.shape[1]
    hv = wv.shape[1]
    half = cos.shape[1]
    tm = _row_tile(s, 512)
    cost = cos.T
    sint = sin.T
    return pl.pallas_call(
        functools.partial(_ret_in_kernel, heads),
        grid=(b, s // tm),
        in_specs=[
            pl.BlockSpec((1, tm, d), lambda i, j: (i, j, 0)),
            pl.BlockSpec((1, d), lambda i, j: (0, 0)),
            pl.BlockSpec((d, hq), lambda i, j: (0, 0)),
            pl.BlockSpec((hq, d), lambda i, j: (0, 0)),
            pl.BlockSpec((d, hv), lambda i, j: (0, 0)),
            pl.BlockSpec((d, hv), lambda i, j: (0, 0)),
            pl.BlockSpec((tm, half), lambda i, j: (j, 0)),
            pl.BlockSpec((tm, half), lambda i, j: (j, 0)),
            pl.BlockSpec((half, tm), lambda i, j: (0, j)),
            pl.BlockSpec((half, tm), lambda i, j: (0, j)),
        ],
        out_specs=[
            pl.BlockSpec((1, tm, hq), lambda i, j: (i, j, 0)),
            pl.BlockSpec((1, hq, tm), lambda i, j: (i, 0, j)),
            pl.BlockSpec((1, tm, hv), lambda i, j: (i, j, 0)),
            pl.BlockSpec((1, tm, hv), lambda i, j: (i, j, 0)),
        ],
        out_shape=[
            jax.ShapeDtypeStruct((b, s, hq), BF16),
            jax.ShapeDtypeStruct((b, hq, s), BF16),
            jax.ShapeDtypeStruct((b, s, hv), BF16),
            jax.ShapeDtypeStruct((b, s, hv), BF16),
        ],
        compiler_params=_cparams("parallel", "parallel"),
        name="ret_in",
    )(x, g, wq, wkt, wv, wg, cos, sin, cost, sint)


def _retention_kernel(chunk, ld_ref, q_ref, kt_ref, v_ref, gate_ref, o_ref, acc_ref, st_ref):
    h = pl.program_id(1)
    s = q_ref.shape[1]
    L = chunk
    nc = s // L
    lgf = ld_ref[0, h]
    lgb = ld_ref[1, h]
    ii = lax.broadcasted_iota(I32, (L, L), 0)
    jj = lax.broadcasted_iota(I32, (L, L), 1)
    dist = (ii - jj).astype(F32)
    causal = ii >= jj
    decay = jnp.where(causal,
                      jnp.exp(lgf * jnp.where(causal, dist, 0.0)),
                      jnp.exp(lgb * jnp.where(causal, 0.0, -dist)))
    col = lax.broadcasted_iota(I32, (L, 1), 0).astype(F32)
    row = lax.broadcasted_iota(I32, (1, L), 1).astype(F32)
    q_dec_f = jnp.exp(lgf * (col + 1.0))
    q_dec_b = jnp.exp(lgb * (L - col))
    k_dec_f = jnp.exp(lgf * (L - 1.0 - row))
    k_dec_b = jnp.exp(lgb * row)
    one = jnp.ones((1, 1), F32)
    c_dec_f = jnp.exp(lgf * L * one)
    c_dec_b = jnp.exp(lgb * L * one)

    def chunk_of(c):
        sl = slice(c * L, (c + 1) * L)
        return q_ref[0, sl, :], kt_ref[0, :, sl], v_ref[0, sl, :]

    for c in range(nc):
        qc, ktc, vc = chunk_of(c)
        scores = (_dot(qc, ktc) * decay).astype(BF16)
        out = _dot(scores, vc)
        if c > 0:
            out += _dot((qc.astype(F32) * q_dec_f).astype(BF16), st_ref[...].astype(BF16))
        acc_ref[c * L:(c + 1) * L, :] = out
        if c < nc - 1:
            upd = _dot((ktc.astype(F32) * k_dec_f).astype(BF16), vc)
            st_ref[...] = upd if c == 0 else st_ref[...] * c_dec_f + upd

    for c in range(nc - 1, -1, -1):
        qc, ktc, vc = chunk_of(c)
        out = acc_ref[c * L:(c + 1) * L, :]
        if c < nc - 1:
            out += _dot((qc.astype(F32) * q_dec_b).astype(BF16), st_ref[...].astype(BF16))
        if c > 0:
            upd = _dot((ktc.astype(F32) * k_dec_b).astype(BF16), vc)
            st_ref[...] = upd if c == nc - 1 else st_ref[...] * c_dec_b + upd
        mu = jnp.mean(out, axis=-1, keepdims=True)
        cen = out - mu
        var = jnp.mean(cen * cen, axis=-1, keepdims=True)
        normed = cen * lax.rsqrt(var + GN_EPS)
        gate = gate_ref[0, c * L:(c + 1) * L, :].astype(F32)
        o_ref[0, c * L:(c + 1) * L, :] = (gate * _sigmoid(gate) * normed).astype(BF16)


def _retention(q, kt, v, gate, log_decay, heads):
    b, s, hq = q.shape
    hv = v.shape[2]
    dk = hq // heads
    dv = hv // heads
    chunk = min(RET_CHUNK, s)
    return pl.pallas_call(
        functools.partial(_retention_kernel, chunk),
        grid=(b, heads),
        in_specs=[
            pl.BlockSpec(memory_space=pltpu.SMEM),
            pl.BlockSpec((1, s, dk), lambda i, h: (i, 0, h)),
            pl.BlockSpec((1, dk, s), lambda i, h: (i, h, 0)),
            pl.BlockSpec((1, s, dv), lambda i, h: (i, 0, h)),
            pl.BlockSpec((1, s, dv), lambda i, h: (i, 0, h)),
        ],
        out_specs=pl.BlockSpec((1, s, dv), lambda i, h: (i, 0, h)),
        out_shape=jax.ShapeDtypeStruct((b, s, hv), BF16),
        scratch_shapes=[
            pltpu.VMEM((s, dv), F32),
            pltpu.VMEM((dk, dv), F32),
        ],
        compiler_params=_cparams("parallel", "parallel"),
        name="retention",
    )(log_decay, q, kt, v, gate)


def _matmul_residual_kernel(a_ref, w_ref, x_ref, o_ref):
    o_ref[...] = x_ref[...] + _dot(a_ref[...], w_ref[...])


def _matmul_residual(a, w, x):
    t, k = a.shape
    d = w.shape[1]
    tm = _row_tile(t, 512)
    return pl.pallas_call(
        _matmul_residual_kernel,
        grid=(t // tm,),
        in_specs=[
            pl.BlockSpec((tm, k), lambda i: (i, 0)),
            pl.BlockSpec((k, d), lambda i: (0, 0)),
            pl.BlockSpec((tm, d), lambda i: (i, 0)),
        ],
        out_specs=pl.BlockSpec((tm, d), lambda i: (i, 0)),
        out_shape=jax.ShapeDtypeStruct((t, d), F32),
        compiler_params=_cparams("parallel"),
        name="ret_out",
    )(a, w, x)


def _router_logits_kernel(x_ref, g_ref, w_ref, hn_ref, l_ref):
    hn = _rms(x_ref[...], g_ref[...])
    hi = hn.astype(BF16)
    lo = (hn - hi.astype(F32)).astype(BF16)
    hn_ref[...] = hi
    l_ref[...] = _dot(hi, w_ref[...]) + _dot(lo, w_ref[...])


def _router_logits(x2d, g, w_split):
    t, d = x2d.shape
    tm = _row_tile(t, 512)
    return pl.pallas_call(
        _router_logits_kernel,
        grid=(t // tm,),
        in_specs=[
            pl.BlockSpec((tm, d), lambda i: (i, 0)),
            pl.BlockSpec((1, d), lambda i: (0, 0)),
            pl.BlockSpec((d, LANES), lambda i: (0, 0)),
        ],
        out_specs=[
            pl.BlockSpec((tm, d), lambda i: (i, 0)),
            pl.BlockSpec((tm, LANES), lambda i: (i, 0)),
        ],
        out_shape=[
            jax.ShapeDtypeStruct((t, d), BF16),
            jax.ShapeDtypeStruct((t, LANES), F32),
        ],
        compiler_params=_cparams("parallel"),
        name="router_logits",
    )(x2d, g, w_split)


def _exclusive_prefix(flags):
    rows, s = flags.shape
    upper = (lax.broadcasted_iota(I32, (LANES, LANES), 0)
             < lax.broadcasted_iota(I32, (LANES, LANES), 1))
    upper = jnp.where(upper, 1.0, 0.0).astype(BF16)
    carry = jnp.zeros((rows, 1), F32)
    pieces = []
    for k in range(s // LANES):
        blk = flags[:, k * LANES:(k + 1) * LANES]
        pieces.append(_dot(blk.astype(BF16), upper) + carry)
        carry = carry + jnp.sum(blk, axis=1, keepdims=True)
    return jnp.concatenate(pieces, axis=1)


def _router_select_kernel(n_exp, cap, l_ref, rank_ref, gate_ref):
    l = l_ref[0]
    logits = l[0:n_exp] + l[n_exp:2 * n_exp] + l[2 * n_exp:3 * n_exp]
    e = jnp.exp(logits - jnp.max(logits, axis=0, keepdims=True))
    aff = e / jnp.sum(e, axis=0, keepdims=True)
    keys = pltpu.bitcast(aff, I32)

    def refine(i, thr):
        cand = thr | jnp.left_shift(jnp.int32(1), 30 - i)
        cnt = jnp.sum(jnp.where(keys >= cand, 1.0, 0.0), axis=1, keepdims=True)
        return jnp.where(cnt >= cap, cand, thr)

    thr = lax.fori_loop(0, 31, refine, jnp.zeros((n_exp, 1), I32))
    above = keys > thr
    tied = keys == thr
    n_above = jnp.sum(jnp.where(above, 1.0, 0.0), axis=1, keepdims=True)
    tied_rank = _exclusive_prefix(jnp.where(tied, 1.0, 0.0))
    keep = above | (tied & (tied_rank < cap - n_above))
    rank = _exclusive_prefix(jnp.where(keep, 1.0, 0.0))
    rank_ref[0] = jnp.where(keep, rank.astype(I32), -1)
    gate_ref[0] = jnp.where(keep, aff, 0.0)


def _router_select(lt, n_exp, cap):
    b, rows, s = lt.shape
    return pl.pallas_call(
        functools.partial(_router_select_kernel, n_exp, cap),
        grid=(b,),
        in_specs=[pl.BlockSpec((1, rows, s), lambda i: (i, 0, 0))],
        out_specs=[
            pl.BlockSpec((1, n_exp, s), lambda i: (i, 0, 0)),
            pl.BlockSpec((1, n_exp, s), lambda i: (i, 0, 0)),
        ],
        out_shape=[
            jax.ShapeDtypeStruct((b, n_exp, s), I32),
            jax.ShapeDtypeStruct((b, n_exp, s), F32),
        ],
        compiler_params=_cparams("parallel"),
        name="router_select",
    )(lt)


def _gather_kernel(cap, hn_ref, rank_ref, gate_ref, xe_ref, gs_ref):
    n_exp = rank_ref.shape[1]
    s = hn_ref.shape[1]
    hn = hn_ref[0]
    slot = lax.broadcasted_iota(I32, (cap, s), 0)
    for e in range(n_exp):
        hit = slot == rank_ref[0, e:e + 1, :]
        xe_ref[e, 0] = _dot(jnp.where(hit, 1.0, 0.0).astype(BF16), hn).astype(BF16)
        g = jnp.sum(jnp.where(hit, gate_ref[0, e:e + 1, :], 0.0), axis=1, keepdims=True)
        gs_ref[e, 0] = jnp.broadcast_to(g, (cap, LANES))


def _gather(hn, rank, gate, cap):
    b, s, d = hn.shape
    n_exp = rank.shape[1]
    return pl.pallas_call(
        functools.partial(_gather_kernel, cap),
        grid=(b,),
        in_specs=[
            pl.BlockSpec((1, s, d), lambda i: (i, 0, 0)),
            pl.BlockSpec((1, n_exp, s), lambda i: (i, 0, 0)),
            pl.BlockSpec((1, n_exp, s), lambda i: (i, 0, 0)),
        ],
        out_specs=[
            pl.BlockSpec((n_exp, 1, cap, d), lambda i: (0, i, 0, 0)),
            pl.BlockSpec((n_exp, 1, cap, LANES), lambda i: (0, i, 0, 0)),
        ],
        out_shape=[
            jax.ShapeDtypeStruct((n_exp, b, cap, d), BF16),
            jax.ShapeDtypeStruct((n_exp, b, cap, LANES), F32),
        ],
        compiler_params=_cparams("parallel"),
        name="moe_gather",
    )(hn, rank, gate)


def _ffn_kernel(f_tile, xe_ref, gs_ref, wg_ref, wu_ref, wd_ref, y_ref):
    x = xe_ref[0]
    ff = wg_ref.shape[2]
    acc = jnp.zeros((x.shape[0], wd_ref.shape[2]), F32)
    for f in range(ff // f_tile):
        sl = slice(f * f_tile, (f + 1) * f_tile)
        a = _dot(x, wg_ref[0, :, sl])
        up = _dot(x, wu_ref[0, :, sl])
        act = (a * _sigmoid(a) * up).astype(BF16)
        acc += _dot(act, wd_ref[0, sl, :])
    y_ref[0] = (acc * gs_ref[0][:, 0:1]).astype(BF16)


def _ffn(xe, gs, wg, wu, wd):
    n_exp, rows, d = xe.shape
    ff = wg.shape[2]
    tm = _row_tile(rows, 1024)
    f_tile = _row_tile(ff, 512)
    return pl.pallas_call(
        functools.partial(_ffn_kernel, f_tile),
        grid=(n_exp, rows // tm),
        in_specs=[
            pl.BlockSpec((1, tm, d), lambda e, m: (e, m, 0)),
            pl.BlockSpec((1, tm, LANES), lambda e, m: (e, m, 0)),
            pl.BlockSpec((1, d, ff), lambda e, m: (e, 0, 0)),
            pl.BlockSpec((1, d, ff), lambda e, m: (e, 0, 0)),
            pl.BlockSpec((1, ff, d), lambda e, m: (e, 0, 0)),
        ],
        out_specs=pl.BlockSpec((1, tm, d), lambda e, m: (e, m, 0)),
        out_shape=jax.ShapeDtypeStruct((n_exp, rows, d), BF16),
        compiler_params=_cparams("parallel", "parallel"),
        name="moe_ffn",
    )(xe, gs, wg, wu, wd)


def _combine_ple_kernel(final, h_ref, y_ref, rankt_ref, p_ref, gple_ref, wgate_ref, wproj_ref,
                        gfin_ref, o_ref):
    n_exp, _, cap, d = y_ref.shape
    ts = h_ref.shape[1]
    rt = rankt_ref[0]
    slot = lax.broadcasted_iota(I32, (ts, cap), 1)
    onehot = jnp.concatenate(
        [jnp.where(rt[:, e:e + 1] == slot, 1.0, 0.0).astype(BF16) for e in range(n_exp)],
        axis=1)
    y = y_ref[:, 0].reshape(n_exp * cap, d)
    h = h_ref[0] + _dot(onehot, y)
    gate = _sigmoid(_dot(_rms(h, gple_ref[...]).astype(BF16), wgate_ref[...]))
    h = h + gate * _dot(p_ref[0].astype(BF16), wproj_ref[...])
    if final:
        h = _rms(h, gfin_ref[...])
    o_ref[0] = h


def _combine_ple(h, y, rankt, p, g_ple, w_gate, w_proj, g_final, final):
    b, s, d = h.shape
    n_exp, _, cap, _ = y.shape
    ple = p.shape[2]
    ts = _row_tile(s, 512)
    return pl.pallas_call(
        functools.partial(_combine_ple_kernel, final),
        grid=(b, s // ts),
        in_specs=[
            pl.BlockSpec((1, ts, d), lambda i, j: (i, j, 0)),
            pl.BlockSpec((n_exp, 1, cap, d), lambda i, j: (0, i, 0, 0)),
            pl.BlockSpec((1, ts, n_exp), lambda i, j: (i, j, 0)),
            pl.BlockSpec((1, ts, ple), lambda i, j: (i, j, 0)),
            pl.BlockSpec((1, d), lambda i, j: (0, 0)),
            pl.BlockSpec((d, d), lambda i, j: (0, 0)),
            pl.BlockSpec((ple, d), lambda i, j: (0, 0)),
            pl.BlockSpec((1, d), lambda i, j: (0, 0)),
        ],
        out_specs=pl.BlockSpec((1, ts, d), lambda i, j: (i, j, 0)),
        out_shape=jax.ShapeDtypeStruct((b, s, d), F32),
        compiler_params=_cparams("parallel", "arbitrary"),
        name="combine_ple",
    )(h, y, rankt, p, g_ple, w_gate, w_proj, g_final)


def _split_router_weight(w):
    d, n_exp = w.shape
    hi = w.astype(BF16)
    r1 = w - hi.astype(F32)
    mid = r1.astype(BF16)
    lo = (r1 - mid.astype(F32)).astype(BF16)
    pad = jnp.zeros((d, LANES - 3 * n_exp), BF16)
    return jnp.concatenate([hi, mid, lo, pad], axis=1)


def _moe_ple(h, p, g_ffn, router_w, w_gate, w_up, w_down, g_ple, ple_w_gate, ple_w_proj,
             g_final, final):
    b, s, d = h.shape
    n_exp = router_w.shape[1]
    cap = EC_CAPACITY_FACTOR * s // n_exp
    hn, lraw = _router_logits(h.reshape(b * s, d), g_ffn, _split_router_weight(router_w))
    lt = jnp.swapaxes(lraw.reshape(b, s, LANES)[:, :, :3 * n_exp], 1, 2)
    rank, gate = _router_select(lt, n_exp, cap)
    xe, gs = _gather(hn.reshape(b, s, d), rank, gate, cap)
    y = _ffn(xe.reshape(n_exp, b * cap, d), gs.reshape(n_exp, b * cap, LANES),
             w_gate.astype(BF16), w_up.astype(BF16), w_down.astype(BF16))
    rankt = jnp.swapaxes(rank, 1, 2)
    return _combine_ple(h, y.reshape(n_exp, b, cap, d), rankt, p, g_ple,
                        ple_w_gate.astype(BF16), ple_w_proj.astype(BF16), g_final, final)


def _rope_tables(s, half):
    inv = ROPE_BASE ** (-jnp.arange(half, dtype=F32) / half)
    ang = jnp.arange(s).astype(F32)[:, None] * inv[None, :]
    return jnp.cos(ang), jnp.sin(ang)


def kernel(x, p, norm_mix, norm_ffn, norm_ple, final_norm, conv_w_in, conv_w, conv_b, conv_w_out,
           ret_w_in, ret_log_decay, ret_w_out, router_w, exp_w_gate, exp_w_up, exp_w_down,
           ple_w_proj, ple_w_gate):
    b, s, d = x.shape
    depth = p.shape[0]
    heads = ret_log_decay.shape[2]
    hq = d
    hv = (ret_w_in.shape[2] - 2 * hq) // 2
    h = x
    for i in range(depth):
        j = i // 2
        g_mix = norm_mix[i][None, :]
        if i % 2 == 0:
            u, bg = _conv_in(h.reshape(b * s, d), g_mix, conv_w_in[j].astype(BF16))
            h = _conv_out(u.reshape(b, s, d), bg.reshape(b, s, d), h, conv_w[j],
                          conv_b[j][None, :], conv_w_out[j].astype(BF16))
        else:
            w_in = ret_w_in[j]
            wq = w_in[:, :hq].astype(BF16)
            wkt = w_in[:, hq:2 * hq].T.astype(BF16)
            wv = w_in[:, 2 * hq:2 * hq + hv].astype(BF16)
            wg = w_in[:, 2 * hq + hv:].astype(BF16)
            cos, sin = _rope_tables(s, hq // heads // 2)
            q, kt, v, gate = _ret_in(h, g_mix, wq, wkt, wv, wg, cos, sin, heads)
            o = _retention(q, kt, v, gate, ret_log_decay[j], heads)
            h = _matmul_residual(o.reshape(b * s, hv), ret_w_out[j].astype(BF16),
                                 h.reshape(b * s, d)).reshape(b, s, d)
        h = _moe_ple(h, p[i], norm_ffn[i][None, :], router_w[i], exp_w_gate[i], exp_w_up[i],
                     exp_w_down[i], norm_ple[i][None, :], ple_w_gate[i], ple_w_proj[i],
                     final_norm[None, :], i == depth - 1)
    return h
```

```python
import functools

import jax
import jax.numpy as jnp
from jax import lax
from jax.experimental import pallas as pl
from jax.experimental.pallas import tpu as pltpu

F32 = jnp.float32
BF16 = jnp.bfloat16
I32 = jnp.int32

NORM_EPS = 1e-6
GN_EPS = 1e-5
ROPE_BASE = 10000.0
EC_CAPACITY_FACTOR = 2
RET_CHUNK = 256
LANES = 128
SUBLANES = 8
VMEM_LIMIT = 56 * 1024 * 1024


def _cparams(*sem):
    return pltpu.CompilerParams(dimension_semantics=sem, vmem_limit_bytes=VMEM_LIMIT)


def _row_tile(n, target):
    t = min(n, target)
    while n % t:
        t //= 2
    return t


def _rms(x, g):
    return x * lax.rsqrt(jnp.mean(x * x, axis=-1, keepdims=True) + NORM_EPS) * g


def _sigmoid(x):
    return 1.0 / (1.0 + jnp.exp(-x))


def _dot(a, b):
    return jnp.dot(a, b, preferred_element_type=F32)


def _router_epilogue(h, g_ref, wr_ref, hn_ref, l_ref):
    hn = _rms(h, g_ref[...])
    hi = hn.astype(BF16)
    lo = (hn - hi.astype(F32)).astype(BF16)
    hn_ref[0] = hi
    l_ref[0] = _dot(hi, wr_ref[...]) + _dot(lo, wr_ref[...])


def _conv_mixer_kernel(xc_ref, xp_ref, g_ref, win_ref, cw_ref, cb_ref, wout_ref, gffn_ref, wr_ref,
                       o_ref, hn_ref, l_ref, u_ref, bg_ref, edge_ref):
    j = pl.program_id(1)
    nj = pl.num_programs(1) - 1
    ts, d = xc_ref.shape[1], xc_ref.shape[2]
    slot = j % 2

    @pl.when(j == 0)
    def _():
        edge_ref[...] = jnp.zeros_like(edge_ref)

    @pl.when(j < nj)
    def _():
        hn = _rms(xc_ref[0], g_ref[...]).astype(BF16)
        bg_ref[slot] = _dot(hn, win_ref[:, :d]).astype(BF16)
        u_ref[slot] = _dot(hn, win_ref[:, d:2 * d]) * _dot(hn, win_ref[:, 2 * d:])

    @pl.when(j >= 1)
    def _():
        u = u_ref[1 - slot]
        prev_row = edge_ref[0:1, :]
        next_row = jnp.where(j < nj, u_ref[slot][0:1, :], 0.0)
        row = lax.broadcasted_iota(I32, u.shape, 0)
        u_prev = jnp.where(row == 0, prev_row, pltpu.roll(u, 1, axis=0))
        u_next = jnp.where(row == ts - 1, next_row, pltpu.roll(u, ts - 1, axis=0))
        cw = cw_ref[...]
        y = cw[0:1, :] * u_prev + cw[1:2, :] * u + cw[2:3, :] * u_next + cb_ref[...]
        z = (bg_ref[1 - slot].astype(F32) * y).astype(BF16)
        h = xp_ref[0] + _dot(z, wout_ref[...])
        o_ref[0] = h
        edge_ref[0:1, :] = u[ts - 1:ts, :]
        _router_epilogue(h, gffn_ref, wr_ref, hn_ref, l_ref)


def _conv_mixer(x, g, w_in, conv_w, conv_b, w_out, g_ffn, w_router):
    b, s, d = x.shape
    ts = _row_tile(s, 512)
    nj = s // ts
    cur = lambda i, j: (i, jnp.minimum(j, nj - 1), 0)
    prev = lambda i, j: (i, jnp.maximum(j - 1, 0), 0)
    const = lambda i, j: (0, 0)
    return pl.pallas_call(
        _conv_mixer_kernel,
        grid=(b, nj + 1),
        in_specs=[
            pl.BlockSpec((1, ts, d), cur),
            pl.BlockSpec((1, ts, d), prev),
            pl.BlockSpec((1, d), const),
            pl.BlockSpec((d, 3 * d), const),
            pl.BlockSpec((3, d), const),
            pl.BlockSpec((1, d), const),
            pl.BlockSpec((d, d), const),
            pl.BlockSpec((1, d), const),
            pl.BlockSpec((d, LANES), const),
        ],
        out_specs=[
            pl.BlockSpec((1, ts, d), prev),
            pl.BlockSpec((1, ts, d), prev),
            pl.BlockSpec((1, ts, LANES), prev),
        ],
        out_shape=[
            jax.ShapeDtypeStruct((b, s, d), F32),
            jax.ShapeDtypeStruct((b, s, d), BF16),
            jax.ShapeDtypeStruct((b, s, LANES), F32),
        ],
        scratch_shapes=[
            pltpu.VMEM((2, ts, d), F32),
            pltpu.VMEM((2, ts, d), BF16),
            pltpu.VMEM((SUBLANES, d), F32),
        ],
        compiler_params=_cparams("parallel", "arbitrary"),
        name="conv_mixer",
    )(x, x, g, w_in, conv_w, conv_b, w_out, g_ffn, w_router)


def _ret_in_kernel(x_ref, g_ref, wq_ref, wkt_ref, wv_ref, wg_ref,
                   cos_ref, sin_ref, cost_ref, sint_ref,
                   q_ref, kt_ref, v_ref, sg_ref):
    heads, _, dk = q_ref.shape[1:]
    dv = v_ref.shape[3]
    half = dk // 2
    hn = _rms(x_ref[0], g_ref[...]).astype(BF16)
    q = _dot(hn, wq_ref[...])
    cos = cos_ref[...]
    sin = sin_ref[...]
    for h in range(heads):
        x1 = q[:, h * dk:h * dk + half]
        x2 = q[:, h * dk + half:(h + 1) * dk]
        q_ref[0, h, :, :half] = (x1 * cos - x2 * sin).astype(BF16)
        q_ref[0, h, :, half:] = (x1 * sin + x2 * cos).astype(BF16)
    kt = lax.dot_general(wkt_ref[...], hn, (((1,), (1,)), ((), ())),
                         preferred_element_type=F32)
    cost = cost_ref[...]
    sint = sint_ref[...]
    scale = dk ** -0.5
    for h in range(heads):
        x1 = kt[h * dk:h * dk + half, :]
        x2 = kt[h * dk + half:(h + 1) * dk, :]
        kt_ref[0, h, :half, :] = ((x1 * cost - x2 * sint) * scale).astype(BF16)
        kt_ref[0, h, half:, :] = ((x1 * sint + x2 * cost) * scale).astype(BF16)
    v = _dot(hn, wv_ref[...])
    gate = _dot(hn, wg_ref[...])
    sg = gate * _sigmoid(gate)
    for h in range(heads):
        v_ref[0, h] = v[:, h * dv:(h + 1) * dv].astype(BF16)
        sg_ref[0, h] = sg[:, h * dv:(h + 1) * dv].astype(BF16)


def _ret_in(x, g, wq, wkt, wv, wg, cos, sin, heads):
    b, s, d = x.shape
    hq = wq.shape[1]
    hv = wv.shape[1]
    dk = hq // heads
    dv = hv // heads
    half = cos.shape[1]
    tm = _row_tile(s, 512)
    cost = cos.T
    sint = sin.T
    const = lambda i, j: (0, 0)
    return pl.pallas_call(
        _ret_in_kernel,
        grid=(b, s // tm),
        in_specs=[
            pl.BlockSpec((1, tm, d), lambda i, j: (i, j, 0)),
            pl.BlockSpec((1, d), const),
            pl.BlockSpec((d, hq), const),
            pl.BlockSpec((hq, d), const),
            pl.BlockSpec((d, hv), const),
            pl.BlockSpec((d, hv), const),
            pl.BlockSpec((tm, half), lambda i, j: (j, 0)),
            pl.BlockSpec((tm, half), lambda i, j: (j, 0)),
            pl.BlockSpec((half, tm), lambda i, j: (0, j)),
            pl.BlockSpec((half, tm), lambda i, j: (0, j)),
        ],
        out_specs=[
            pl.BlockSpec((1, heads, tm, dk), lambda i, j: (i, 0, j, 0)),
            pl.BlockSpec((1, heads, dk, tm), lambda i, j: (i, 0, 0, j)),
            pl.BlockSpec((1, heads, tm, dv), lambda i, j: (i, 0, j, 0)),
            pl.BlockSpec((1, heads, tm, dv), lambda i, j: (i, 0, j, 0)),
        ],
        out_shape=[
            jax.ShapeDtypeStruct((b, heads, s, dk), BF16),
            jax.ShapeDtypeStruct((b, heads, dk, s), BF16),
            jax.ShapeDtypeStruct((b, heads, s, dv), BF16),
            jax.ShapeDtypeStruct((b, heads, s, dv), BF16),
        ],
        compiler_params=_cparams("parallel", "parallel"),
        name="ret_in",
    )(x, g, wq, wkt, wv, wg, cos, sin, cost, sint)


def _retention_kernel(chunk, ld_ref, q_ref, kt_ref, v_ref, sg_ref, o_ref, acc_ref, sf_ref, sb_ref):
    h = pl.program_id(1)
    s = q_ref.shape[2]
    L = chunk
    nc = s // L
    lgf = ld_ref[0, h]
    lgb = ld_ref[1, h]
    ii = lax.broadcasted_iota(I32, (L, L), 0)
    jj = lax.broadcasted_iota(I32, (L, L), 1)
    dist = (ii - jj).astype(F32)
    causal = ii >= jj
    decay = jnp.where(causal,
                      jnp.exp(lgf * jnp.where(causal, dist, 0.0)),
                      jnp.exp(lgb * jnp.where(causal, 0.0, -dist)))
    col = lax.broadcasted_iota(I32, (L, 1), 0).astype(F32)
    row = lax.broadcasted_iota(I32, (1, L), 1).astype(F32)
    q_dec_f = jnp.exp(lgf * (col + 1.0)).astype(BF16)
    q_dec_b = jnp.exp(lgb * (L - col)).astype(BF16)
    k_dec_f = jnp.exp(lgf * (L - 1.0 - row)).astype(BF16)
    k_dec_b = jnp.exp(lgb * row).astype(BF16)
    one = jnp.ones((1, 1), F32)
    c_dec_f = jnp.exp(lgf * L * one)
    c_dec_b = jnp.exp(lgb * L * one)

    def finish(c, out):
        sl = slice(c * L, (c + 1) * L)
        mu = jnp.mean(out, axis=-1, keepdims=True)
        cen = out - mu
        var = jnp.mean(cen * cen, axis=-1, keepdims=True)
        o_ref[0, 0, sl, :] = (sg_ref[0, 0, sl, :].astype(F32) * (cen * lax.rsqrt(var + GN_EPS))
                              ).astype(BF16)

    stored = {}

    def contribute(c, part):
        sl = slice(c * L, (c + 1) * L)
        if c not in stored:
            stored[c] = part is not None
            if part is not None:
                acc_ref[sl, :] = part
        elif not stored[c]:
            finish(c, part)
        else:
            finish(c, acc_ref[sl, :] if part is None else acc_ref[sl, :] + part)

    for k in range(nc):
        cf, cb = k, nc - 1 - k
        slf = slice(cf * L, (cf + 1) * L)
        slb = slice(cb * L, (cb + 1) * L)
        qc, ktc, vc = q_ref[0, 0, slf, :], kt_ref[0, 0, :, slf], v_ref[0, 0, slf, :]
        scores = (_dot(qc, ktc) * decay).astype(BF16)
        part_f = _dot(scores, vc)
        if cf > 0:
            part_f += _dot(qc * q_dec_f, sf_ref[...].astype(BF16))
        if cf < nc - 1:
            upd = _dot(ktc * k_dec_f, vc)
            sf_ref[...] = upd if cf == 0 else sf_ref[...] * c_dec_f + upd
        part_b = None
        if cb < nc - 1:
            part_b = _dot(q_ref[0, 0, slb, :] * q_dec_b, sb_ref[...].astype(BF16))
        if cb > 0:
            upd = _dot(kt_ref[0, 0, :, slb] * k_dec_b, v_ref[0, 0, slb, :])
            sb_ref[...] = upd if cb == nc - 1 else sb_ref[...] * c_dec_b + upd
        contribute(cf, part_f)
        contribute(cb, part_b)


def _retention(q, kt, v, sg, log_decay):
    b, heads, s, dk = q.shape
    dv = v.shape[3]
    chunk = min(RET_CHUNK, s)
    return pl.pallas_call(
        functools.partial(_retention_kernel, chunk),
        grid=(b, heads),
        in_specs=[
            pl.BlockSpec(memory_space=pltpu.SMEM),
            pl.BlockSpec((1, 1, s, dk), lambda i, h: (i, h, 0, 0)),
            pl.BlockSpec((1, 1, dk, s), lambda i, h: (i, h, 0, 0)),
            pl.BlockSpec((1, 1, s, dv), lambda i, h: (i, h, 0, 0)),
            pl.BlockSpec((1, 1, s, dv), lambda i, h: (i, h, 0, 0)),
        ],
        out_specs=pl.BlockSpec((1, 1, s, dv), lambda i, h: (i, h, 0, 0)),
        out_shape=jax.ShapeDtypeStruct((b, heads, s, dv), BF16),
        scratch_shapes=[
            pltpu.VMEM((s, dv), F32),
            pltpu.VMEM((dk, dv), F32),
            pltpu.VMEM((dk, dv), F32),
        ],
        compiler_params=_cparams("parallel", "parallel"),
        name="retention",
    )(log_decay, q, kt, v, sg)


def _ret_out_kernel(o_ref, w_ref, x_ref, gffn_ref, wr_ref, h_ref, hn_ref, l_ref):
    heads = o_ref.shape[1]
    h = x_ref[0]
    for hd in range(heads):
        h = h + _dot(o_ref[0, hd], w_ref[hd])
    h_ref[0] = h
    _router_epilogue(h, gffn_ref, wr_ref, hn_ref, l_ref)


def _ret_out(o, w_out, x, g_ffn, w_router):
    b, heads, s, dv = o.shape
    d = x.shape[2]
    ts = _row_tile(s, 512)
    tile = lambda i, j: (i, j, 0)
    const = lambda i, j: (0, 0)
    return pl.pallas_call(
        _ret_out_kernel,
        grid=(b, s // ts),
        in_specs=[
            pl.BlockSpec((1, heads, ts, dv), lambda i, j: (i, 0, j, 0)),
            pl.BlockSpec((heads, dv, d), lambda i, j: (0, 0, 0)),
            pl.BlockSpec((1, ts, d), tile),
            pl.BlockSpec((1, d), const),
            pl.BlockSpec((d, LANES), const),
        ],
        out_specs=[
            pl.BlockSpec((1, ts, d), tile),
            pl.BlockSpec((1, ts, d), tile),
            pl.BlockSpec((1, ts, LANES), tile),
        ],
        out_shape=[
            jax.ShapeDtypeStruct((b, s, d), F32),
            jax.ShapeDtypeStruct((b, s, d), BF16),
            jax.ShapeDtypeStruct((b, s, LANES), F32),
        ],
        compiler_params=_cparams("parallel", "parallel"),
        name="ret_out",
    )(o, w_out.reshape(heads, dv, d), x, g_ffn, w_router)


def _exclusive_prefix(flags):
    rows, s = flags.shape
    upper = (lax.broadcasted_iota(I32, (LANES, LANES), 0)
             < lax.broadcasted_iota(I32, (LANES, LANES), 1))
    upper = jnp.where(upper, 1.0, 0.0).astype(BF16)
    carry = jnp.zeros((rows, 1), F32)
    pieces = []
    for k in range(s // LANES):
        blk = flags[:, k * LANES:(k + 1) * LANES]
        pieces.append(_dot(blk.astype(BF16), upper) + carry)
        carry = carry + jnp.sum(blk, axis=1, keepdims=True)
    return jnp.concatenate(pieces, axis=1)


def _router_select_kernel(n_exp, cap, l_ref, rank_ref, gate_ref):
    nb = l_ref.shape[0]
    affs = []
    for i in range(nb):
        l = l_ref[i]
        logits = l[0:n_exp] + l[n_exp:2 * n_exp] + l[2 * n_exp:3 * n_exp]
        e = jnp.exp(logits - jnp.max(logits, axis=0, keepdims=True))
        affs.append(e / jnp.sum(e, axis=0, keepdims=True))
    aff = jnp.concatenate(affs, axis=0)
    keys = pltpu.bitcast(aff, I32)

    def refine(i, thr):
        cand = thr | jnp.left_shift(jnp.int32(1), 30 - i)
        cnt = jnp.sum(jnp.where(keys >= cand, 1.0, 0.0), axis=1, keepdims=True)
        return jnp.where(cnt >= cap, cand, thr)

    thr = lax.fori_loop(0, 31, refine, jnp.zeros((nb * n_exp, 1), I32))
    above = keys > thr
    tied = keys == thr
    n_above = jnp.sum(jnp.where(above, 1.0, 0.0), axis=1, keepdims=True)
    tied_rank = _exclusive_prefix(jnp.where(tied, 1.0, 0.0))
    keep = above | (tied & (tied_rank < cap - n_above))
    rank = _exclusive_prefix(jnp.where(keep, 1.0, 0.0))
    rank = jnp.where(keep, rank.astype(I32), -1)
    gate = jnp.where(keep, aff, 0.0)
    for i in range(nb):
        rank_ref[i] = rank[i * n_exp:(i + 1) * n_exp]
        gate_ref[i] = gate[i * n_exp:(i + 1) * n_exp]


def _router_select(lt, n_exp, cap):
    b, rows, s = lt.shape
    nb = _row_tile(b, 4)
    blk = lambda i: (i, 0, 0)
    return pl.pallas_call(
        functools.partial(_router_select_kernel, n_exp, cap),
        grid=(b // nb,),
        in_specs=[pl.BlockSpec((nb, rows, s), blk)],
        out_specs=[
            pl.BlockSpec((nb, n_exp, s), blk),
            pl.BlockSpec((nb, n_exp, s), blk),
        ],
        out_shape=[
            jax.ShapeDtypeStruct((b, n_exp, s), I32),
            jax.ShapeDtypeStruct((b, n_exp, s), F32),
        ],
        compiler_params=_cparams("parallel"),
        name="router_select",
    )(lt)


def _gather_kernel(cap, hn_ref, rank_ref, gate_ref, xe_ref, gs_ref):
    n_exp = rank_ref.shape[1]
    s = hn_ref.shape[1]
    hn = hn_ref[0]
    slot = lax.broadcasted_iota(I32, (cap, s), 0)
    for e in range(n_exp):
        hit = slot == rank_ref[0, e:e + 1, :]
        xe_ref[e, 0] = _dot(jnp.where(hit, 1.0, 0.0).astype(BF16), hn).astype(BF16)
        g = jnp.sum(jnp.where(hit, gate_ref[0, e:e + 1, :], 0.0), axis=1, keepdims=True)
        gs_ref[e, 0] = jnp.broadcast_to(g, (cap, LANES))


def _gather(hn, rank, gate, cap):
    b, s, d = hn.shape
    n_exp = rank.shape[1]
    return pl.pallas_call(
        functools.partial(_gather_kernel, cap),
        grid=(b,),
        in_specs=[
            pl.BlockSpec((1, s, d), lambda i: (i, 0, 0)),
            pl.BlockSpec((1, n_exp, s), lambda i: (i, 0, 0)),
            pl.BlockSpec((1, n_exp, s), lambda i: (i, 0, 0)),
        ],
        out_specs=[
            pl.BlockSpec((n_exp, 1, cap, d), lambda i: (0, i, 0, 0)),
            pl.BlockSpec((n_exp, 1, cap, LANES), lambda i: (0, i, 0, 0)),
        ],
        out_shape=[
            jax.ShapeDtypeStruct((n_exp, b, cap, d), BF16),
            jax.ShapeDtypeStruct((n_exp, b, cap, LANES), F32),
        ],
        compiler_params=_cparams("parallel"),
        name="moe_gather",
    )(hn, rank, gate)


def _ffn_kernel(f_tile, xe_ref, gs_ref, wg_ref, wu_ref, wd_ref, y_ref):
    x = xe_ref[0]
    ff = wg_ref.shape[1]
    acc = jnp.zeros((x.shape[0], wd_ref.shape[1]), F32)
    for f in range(ff // f_tile):
        sl = slice(f * f_tile, (f + 1) * f_tile)
        a = _dot(x, wg_ref[:, sl])
        up = _dot(x, wu_ref[:, sl])
        act = (a * _sigmoid(a) * up).astype(BF16)
        acc += _dot(act, wd_ref[sl, :])
    y_ref[0] = (acc * gs_ref[0][:, 0:1]).astype(BF16)


def _ffn(layer, xe, gs, wg, wu, wd):
    n_exp, rows, d = xe.shape
    ff = wg.shape[3]
    tm = _row_tile(rows, 1024)
    f_tile = _row_tile(ff, 512)
    return pl.pallas_call(
        functools.partial(_ffn_kernel, f_tile),
        grid=(n_exp, rows // tm),
        in_specs=[
            pl.BlockSpec((1, tm, d), lambda e, m: (e, m, 0)),
            pl.BlockSpec((1, tm, LANES), lambda e, m: (e, m, 0)),
            pl.BlockSpec((None, None, d, ff), lambda e, m: (layer, e, 0, 0)),
            pl.BlockSpec((None, None, d, ff), lambda e, m: (layer, e, 0, 0)),
            pl.BlockSpec((None, None, ff, d), lambda e, m: (layer, e, 0, 0)),
        ],
        out_specs=pl.BlockSpec((1, tm, d), lambda e, m: (e, m, 0)),
        out_shape=jax.ShapeDtypeStruct((n_exp, rows, d), BF16),
        compiler_params=_cparams("parallel", "parallel"),
        name="moe_ffn",
    )(xe, gs, wg, wu, wd)


def _combine_ple_kernel(final, h_ref, y_ref, rankt_ref, p_ref, gple_ref, wgate_ref, wproj_ref,
                        gfin_ref, o_ref):
    n_exp, _, cap, d = y_ref.shape
    ts = h_ref.shape[1]
    rt = rankt_ref[0]
    slot = lax.broadcasted_iota(I32, (ts, cap), 1)
    onehot = jnp.concatenate(
        [jnp.where(rt[:, e:e + 1] == slot, 1.0, 0.0).astype(BF16) for e in range(n_exp)],
        axis=1)
    y = y_ref[:, 0].reshape(n_exp * cap, d)
    h = h_ref[0] + _dot(onehot, y)
    gate = _sigmoid(_dot(_rms(h, gple_ref[...]).astype(BF16), wgate_ref[...]))
    h = h + gate * _dot(p_ref[...].astype(BF16), wproj_ref[...])
    if final:
        h = _rms(h, gfin_ref[...])
    o_ref[0] = h


def _combine_ple(layer, h, y, rankt, p, g_ple, w_gate, w_proj, g_final, final):
    b, s, d = h.shape
    n_exp, _, cap, _ = y.shape
    ple = p.shape[3]
    ts = _row_tile(s, 512)
    const = lambda i, j: (0, 0)
    return pl.pallas_call(
        functools.partial(_combine_ple_kernel, final),
        grid=(b, s // ts),
        in_specs=[
            pl.BlockSpec((1, ts, d), lambda i, j: (i, j, 0)),
            pl.BlockSpec((n_exp, 1, cap, d), lambda i, j: (0, i, 0, 0)),
            pl.BlockSpec((1, ts, n_exp), lambda i, j: (i, j, 0)),
            pl.BlockSpec((None, None, ts, ple), lambda i, j: (layer, i, j, 0)),
            pl.BlockSpec((1, d), const),
            pl.BlockSpec((d, d), const),
            pl.BlockSpec((ple, d), const),
            pl.BlockSpec((1, d), const),
        ],
        out_specs=pl.BlockSpec((1, ts, d), lambda i, j: (i, j, 0)),
        out_shape=jax.ShapeDtypeStruct((b, s, d), F32),
        compiler_params=_cparams("parallel", "arbitrary"),
        name="combine_ple",
    )(h, y, rankt, p, g_ple, w_gate, w_proj, g_final)


def _split_router_weight(w):
    d, n_exp = w.shape
    hi = w.astype(BF16)
    r1 = w - hi.astype(F32)
    mid = r1.astype(BF16)
    lo = (r1 - mid.astype(F32)).astype(BF16)
    pad = jnp.zeros((d, LANES - 3 * n_exp), BF16)
    return jnp.concatenate([hi, mid, lo, pad], axis=1)


def _moe_ple(layer, h, hn, lraw, p, n_exp, w_gate, w_up, w_down, g_ple, ple_w_gate, ple_w_proj,
             g_final, final):
    b, s, d = h.shape
    cap = EC_CAPACITY_FACTOR * s // n_exp
    lt = jnp.swapaxes(lraw[:, :, :3 * n_exp], 1, 2)
    rank, gate = _router_select(lt, n_exp, cap)
    xe, gs = _gather(hn, rank, gate, cap)
    y = _ffn(layer, xe.reshape(n_exp, b * cap, d), gs.reshape(n_exp, b * cap, LANES),
             w_gate, w_up, w_down)
    rankt = jnp.swapaxes(rank, 1, 2)
    return _combine_ple(layer, h, y.reshape(n_exp, b, cap, d), rankt, p, g_ple,
                        ple_w_gate, ple_w_proj, g_final, final)


def _rope_tables(s, half):
    inv = ROPE_BASE ** (-jnp.arange(half, dtype=F32) / half)
    ang = jnp.arange(s).astype(F32)[:, None] * inv[None, :]
    return jnp.cos(ang), jnp.sin(ang)


def kernel(x, p, norm_mix, norm_ffn, norm_ple, final_norm, conv_w_in, conv_w, conv_b, conv_w_out,
           ret_w_in, ret_log_decay, ret_w_out, router_w, exp_w_gate, exp_w_up, exp_w_down,
           ple_w_proj, ple_w_gate):
    b, s, d = x.shape
    depth = p.shape[0]
    heads = ret_log_decay.shape[2]
    n_exp = router_w.shape[2]
    hq = d
    hv = (ret_w_in.shape[2] - 2 * hq) // 2
    w_gate = exp_w_gate.astype(BF16)
    w_up = exp_w_up.astype(BF16)
    w_down = exp_w_down.astype(BF16)
    h = x
    for i in range(depth):
        j = i // 2
        g_mix = norm_mix[i][None, :]
        g_ffn = norm_ffn[i][None, :]
        w_router = _split_router_weight(router_w[i])
        if i % 2 == 0:
            h, hn, lraw = _conv_mixer(h, g_mix, conv_w_in[j].astype(BF16), conv_w[j],
                                      conv_b[j][None, :], conv_w_out[j].astype(BF16),
                                      g_ffn, w_router)
        else:
            w_in = ret_w_in[j]
            wq = w_in[:, :hq].astype(BF16)
            wkt = w_in[:, hq:2 * hq].T.astype(BF16)
            wv = w_in[:, 2 * hq:2 * hq + hv].astype(BF16)
            wg = w_in[:, 2 * hq + hv:].astype(BF16)
            cos, sin = _rope_tables(s, hq // heads // 2)
            q, kt, v, sg = _ret_in(h, g_mix, wq, wkt, wv, wg, cos, sin, heads)
            o = _retention(q, kt, v, sg, ret_log_decay[j])
            h, hn, lraw = _ret_out(o, ret_w_out[j].astype(BF16), h, g_ffn, w_router)
        h = _moe_ple(i, h, hn, lraw, p, n_exp, w_gate, w_up, w_down, norm_ple[i][None, :],
                     ple_w_gate[i].astype(BF16), ple_w_proj[i].astype(BF16),
                     final_norm[None, :], i == depth - 1)
    return h
```

```python
import functools

import jax
import jax.numpy as jnp
from jax import lax
from jax.experimental import pallas as pl
from jax.experimental.pallas import tpu as pltpu

F32 = jnp.float32
BF16 = jnp.bfloat16
I32 = jnp.int32

NORM_EPS = 1e-6
GN_EPS = 1e-5
ROPE_BASE = 10000.0
EC_CAPACITY_FACTOR = 2
RET_CHUNK = 256
LANES = 128
SUBLANES = 8
VMEM_LIMIT = 56 * 1024 * 1024


def _cparams(*sem):
    return pltpu.CompilerParams(dimension_semantics=sem, vmem_limit_bytes=VMEM_LIMIT)


def _row_tile(n, target):
    t = min(n, target)
    while n % t:
        t //= 2
    return t


def _rms(x, g):
    return x * lax.rsqrt(jnp.mean(x * x, axis=-1, keepdims=True) + NORM_EPS) * g


def _sigmoid(x):
    return 1.0 / (1.0 + jnp.exp(-x))


def _dot(a, b):
    return jnp.dot(a, b, preferred_element_type=F32)


def _router_epilogue(h, g_ref, wr_ref, hn_ref, l_ref, rows=slice(None)):
    hn = _rms(h, g_ref[...])
    hi = hn.astype(BF16)
    lo = (hn - hi.astype(F32)).astype(BF16)
    hn_ref[0, rows, :] = hi
    l_ref[0, rows, :] = _dot(hi, wr_ref[...]) + _dot(lo, wr_ref[...])


def _conv_mixer_kernel(xc_ref, xp_ref, g_ref, win_ref, cw_ref, cb_ref, wout_ref, gffn_ref, wr_ref,
                       o_ref, hn_ref, l_ref, u_ref, bg_ref, edge_ref):
    j = pl.program_id(1)
    nj = pl.num_programs(1) - 1
    ts, d = xc_ref.shape[1], xc_ref.shape[2]
    slot = j % 2

    @pl.when(j == 0)
    def _():
        edge_ref[...] = jnp.zeros_like(edge_ref)

    @pl.when(j < nj)
    def _():
        hn = _rms(xc_ref[0], g_ref[...]).astype(BF16)
        bg_ref[slot] = _dot(hn, win_ref[:, :d]).astype(BF16)
        u_ref[slot] = _dot(hn, win_ref[:, d:2 * d]) * _dot(hn, win_ref[:, 2 * d:])

    @pl.when(j >= 1)
    def _():
        u = u_ref[1 - slot]
        prev_row = edge_ref[0:1, :]
        next_row = jnp.where(j < nj, u_ref[slot][0:1, :], 0.0)
        row = lax.broadcasted_iota(I32, u.shape, 0)
        u_prev = jnp.where(row == 0, prev_row, pltpu.roll(u, 1, axis=0))
        u_next = jnp.where(row == ts - 1, next_row, pltpu.roll(u, ts - 1, axis=0))
        cw = cw_ref[...]
        edge_ref[0:1, :] = u[ts - 1:ts, :]
        sub = ts // 2
        for r in range(0, ts, sub):
            rows = slice(r, r + sub)
            y = (cw[0:1, :] * u_prev[rows] + cw[1:2, :] * u[rows] + cw[2:3, :] * u_next[rows]
                 + cb_ref[...])
            z = (bg_ref[1 - slot, rows, :].astype(F32) * y).astype(BF16)
            h = xp_ref[0, rows, :] + _dot(z, wout_ref[...])
            o_ref[0, rows, :] = h
            _router_epilogue(h, gffn_ref, wr_ref, hn_ref, l_ref, rows)


def _conv_mixer(x, g, w_in, conv_w, conv_b, w_out, g_ffn, w_router):
    b, s, d = x.shape
    ts = _row_tile(s, 512)
    nj = s // ts
    cur = lambda i, j: (i, jnp.minimum(j, nj - 1), 0)
    prev = lambda i, j: (i, jnp.maximum(j - 1, 0), 0)
    const = lambda i, j: (0, 0)
    return pl.pallas_call(
        _conv_mixer_kernel,
        grid=(b, nj + 1),
        in_specs=[
            pl.BlockSpec((1, ts, d), cur),
            pl.BlockSpec((1, ts, d), prev),
            pl.BlockSpec((1, d), const),
            pl.BlockSpec((d, 3 * d), const),
            pl.BlockSpec((3, d), const),
            pl.BlockSpec((1, d), const),
            pl.BlockSpec((d, d), const),
            pl.BlockSpec((1, d), const),
            pl.BlockSpec((d, LANES), const),
        ],
        out_specs=[
            pl.BlockSpec((1, ts, d), prev),
            pl.BlockSpec((1, ts, d), prev),
            pl.BlockSpec((1, ts, LANES), prev),
        ],
        out_shape=[
            jax.ShapeDtypeStruct((b, s, d), F32),
            jax.ShapeDtypeStruct((b, s, d), BF16),
            jax.ShapeDtypeStruct((b, s, LANES), F32),
        ],
        scratch_shapes=[
            pltpu.VMEM((2, ts, d), F32),
            pltpu.VMEM((2, ts, d), BF16),
            pltpu.VMEM((SUBLANES, d), F32),
        ],
        compiler_params=_cparams("parallel", "arbitrary"),
        name="conv_mixer",
    )(x, x, g, w_in, conv_w, conv_b, w_out, g_ffn, w_router)


def _ret_in_kernel(x_ref, g_ref, wq_ref, wkt_ref, wv_ref, wg_ref,
                   cos_ref, sin_ref, cost_ref, sint_ref,
                   q_ref, kt_ref, v_ref, sg_ref):
    heads, _, dk = q_ref.shape[1:]
    dv = v_ref.shape[3]
    half = dk // 2
    hn = _rms(x_ref[0], g_ref[...]).astype(BF16)
    q = _dot(hn, wq_ref[...])
    cos = cos_ref[...]
    sin = sin_ref[...]
    for h in range(heads):
        x1 = q[:, h * dk:h * dk + half]
        x2 = q[:, h * dk + half:(h + 1) * dk]
        q_ref[0, h, :, :half] = (x1 * cos - x2 * sin).astype(BF16)
        q_ref[0, h, :, half:] = (x1 * sin + x2 * cos).astype(BF16)
    kt = lax.dot_general(wkt_ref[...], hn, (((1,), (1,)), ((), ())),
                         preferred_element_type=F32)
    cost = cost_ref[...]
    sint = sint_ref[...]
    scale = dk ** -0.5
    for h in range(heads):
        x1 = kt[h * dk:h * dk + half, :]
        x2 = kt[h * dk + half:(h + 1) * dk, :]
        kt_ref[0, h, :half, :] = ((x1 * cost - x2 * sint) * scale).astype(BF16)
        kt_ref[0, h, half:, :] = ((x1 * sint + x2 * cost) * scale).astype(BF16)
    v = _dot(hn, wv_ref[...])
    gate = _dot(hn, wg_ref[...])
    sg = gate * _sigmoid(gate)
    for h in range(heads):
        v_ref[0, h] = v[:, h * dv:(h + 1) * dv].astype(BF16)
        sg_ref[0, h] = sg[:, h * dv:(h + 1) * dv].astype(BF16)


def _ret_in(x, g, wq, wkt, wv, wg, cos, sin, heads):
    b, s, d = x.shape
    hq = wq.shape[1]
    hv = wv.shape[1]
    dk = hq // heads
    dv = hv // heads
    half = cos.shape[1]
    tm = _row_tile(s, 512)
    cost = cos.T
    sint = sin.T
    const = lambda i, j: (0, 0)
    return pl.pallas_call(
        _ret_in_kernel,
        grid=(b, s // tm),
        in_specs=[
            pl.BlockSpec((1, tm, d), lambda i, j: (i, j, 0)),
            pl.BlockSpec((1, d), const),
            pl.BlockSpec((d, hq), const),
            pl.BlockSpec((hq, d), const),
            pl.BlockSpec((d, hv), const),
            pl.BlockSpec((d, hv), const),
            pl.BlockSpec((tm, half), lambda i, j: (j, 0)),
            pl.BlockSpec((tm, half), lambda i, j: (j, 0)),
            pl.BlockSpec((half, tm), lambda i, j: (0, j)),
            pl.BlockSpec((half, tm), lambda i, j: (0, j)),
        ],
        out_specs=[
            pl.BlockSpec((1, heads, tm, dk), lambda i, j: (i, 0, j, 0)),
            pl.BlockSpec((1, heads, dk, tm), lambda i, j: (i, 0, 0, j)),
            pl.BlockSpec((1, heads, tm, dv), lambda i, j: (i, 0, j, 0)),
            pl.BlockSpec((1, heads, tm, dv), lambda i, j: (i, 0, j, 0)),
        ],
        out_shape=[
            jax.ShapeDtypeStruct((b, heads, s, dk), BF16),
            jax.ShapeDtypeStruct((b, heads, dk, s), BF16),
            jax.ShapeDtypeStruct((b, heads, s, dv), BF16),
            jax.ShapeDtypeStruct((b, heads, s, dv), BF16),
        ],
        compiler_params=_cparams("parallel", "parallel"),
        name="ret_in",
    )(x, g, wq, wkt, wv, wg, cos, sin, cost, sint)


def _retention_kernel(chunk, ld_ref, q_ref, kt_ref, v_ref, sg_ref, o_ref, acc_ref, sf_ref, sb_ref):
    h = pl.program_id(1)
    s = q_ref.shape[2]
    L = chunk
    nc = s // L
    lgf = ld_ref[0, h]
    lgb = ld_ref[1, h]
    ii = lax.broadcasted_iota(I32, (L, L), 0)
    jj = lax.broadcasted_iota(I32, (L, L), 1)
    dist = (ii - jj).astype(F32)
    causal = ii >= jj
    decay = jnp.where(causal,
                      jnp.exp(lgf * jnp.where(causal, dist, 0.0)),
                      jnp.exp(lgb * jnp.where(causal, 0.0, -dist)))
    col = lax.broadcasted_iota(I32, (L, 1), 0).astype(F32)
    row = lax.broadcasted_iota(I32, (1, L), 1).astype(F32)
    q_dec_f = jnp.exp(lgf * (col + 1.0)).astype(BF16)
    q_dec_b = jnp.exp(lgb * (L - col)).astype(BF16)
    k_dec_f = jnp.exp(lgf * (L - 1.0 - row)).astype(BF16)
    k_dec_b = jnp.exp(lgb * row).astype(BF16)
    one = jnp.ones((1, 1), F32)
    c_dec_f = jnp.exp(lgf * L * one)
    c_dec_b = jnp.exp(lgb * L * one)

    def finish(c, out):
        sl = slice(c * L, (c + 1) * L)
        mu = jnp.mean(out, axis=-1, keepdims=True)
        cen = out - mu
        var = jnp.mean(cen * cen, axis=-1, keepdims=True)
        o_ref[0, 0, sl, :] = (sg_ref[0, 0, sl, :].astype(F32) * (cen * lax.rsqrt(var + GN_EPS))
                              ).astype(BF16)

    stored = {}

    def contribute(c, part):
        sl = slice(c * L, (c + 1) * L)
        if c not in stored:
            stored[c] = part is not None
            if part is not None:
                acc_ref[sl, :] = part
        elif not stored[c]:
            finish(c, part)
        else:
            finish(c, acc_ref[sl, :] if part is None else acc_ref[sl, :] + part)

    for k in range(nc):
        cf, cb = k, nc - 1 - k
        slf = slice(cf * L, (cf + 1) * L)
        slb = slice(cb * L, (cb + 1) * L)
        qc, ktc, vc = q_ref[0, 0, slf, :], kt_ref[0, 0, :, slf], v_ref[0, 0, slf, :]
        scores = (_dot(qc, ktc) * decay).astype(BF16)
        part_f = _dot(scores, vc)
        if cf > 0:
            part_f += _dot(qc * q_dec_f, sf_ref[...].astype(BF16))
        if cf < nc - 1:
            upd = _dot(ktc * k_dec_f, vc)
            sf_ref[...] = upd if cf == 0 else sf_ref[...] * c_dec_f + upd
        part_b = None
        if cb < nc - 1:
            part_b = _dot(q_ref[0, 0, slb, :] * q_dec_b, sb_ref[...].astype(BF16))
        if cb > 0:
            upd = _dot(kt_ref[0, 0, :, slb] * k_dec_b, v_ref[0, 0, slb, :])
            sb_ref[...] = upd if cb == nc - 1 else sb_ref[...] * c_dec_b + upd
        contribute(cf, part_f)
        contribute(cb, part_b)


def _retention(q, kt, v, sg, log_decay):
    b, heads, s, dk = q.shape
    dv = v.shape[3]
    chunk = min(RET_CHUNK, s)
    return pl.pallas_call(
        functools.partial(_retention_kernel, chunk),
        grid=(b, heads),
        in_specs=[
            pl.BlockSpec(memory_space=pltpu.SMEM),
            pl.BlockSpec((1, 1, s, dk), lambda i, h: (i, h, 0, 0)),
            pl.BlockSpec((1, 1, dk, s), lambda i, h: (i, h, 0, 0)),
            pl.BlockSpec((1, 1, s, dv), lambda i, h: (i, h, 0, 0)),
            pl.BlockSpec((1, 1, s, dv), lambda i, h: (i, h, 0, 0)),
        ],
        out_specs=pl.BlockSpec((1, 1, s, dv), lambda i, h: (i, h, 0, 0)),
        out_shape=jax.ShapeDtypeStruct((b, heads, s, dv), BF16),
        scratch_shapes=[
            pltpu.VMEM((s, dv), F32),
            pltpu.VMEM((dk, dv), F32),
            pltpu.VMEM((dk, dv), F32),
        ],
        compiler_params=_cparams("parallel", "parallel"),
        name="retention",
    )(log_decay, q, kt, v, sg)


def _ret_out_kernel(o_ref, w_ref, x_ref, gffn_ref, wr_ref, h_ref, hn_ref, l_ref):
    heads, ts = o_ref.shape[1], o_ref.shape[2]
    sub = ts // 2
    for r in range(0, ts, sub):
        rows = slice(r, r + sub)
        h = x_ref[0, rows, :]
        for hd in range(heads):
            h = h + _dot(o_ref[0, hd, rows, :], w_ref[hd])
        h_ref[0, rows, :] = h
        _router_epilogue(h, gffn_ref, wr_ref, hn_ref, l_ref, rows)


def _ret_out(o, w_out, x, g_ffn, w_router):
    b, heads, s, dv = o.shape
    d = x.shape[2]
    ts = _row_tile(s, 512)
    tile = lambda i, j: (i, j, 0)
    const = lambda i, j: (0, 0)
    return pl.pallas_call(
        _ret_out_kernel,
        grid=(b, s // ts),
        in_specs=[
            pl.BlockSpec((1, heads, ts, dv), lambda i, j: (i, 0, j, 0)),
            pl.BlockSpec((heads, dv, d), lambda i, j: (0, 0, 0)),
            pl.BlockSpec((1, ts, d), tile),
            pl.BlockSpec((1, d), const),
            pl.BlockSpec((d, LANES), const),
        ],
        out_specs=[
            pl.BlockSpec((1, ts, d), tile),
            pl.BlockSpec((1, ts, d), tile),
            pl.BlockSpec((1, ts, LANES), tile),
        ],
        out_shape=[
            jax.ShapeDtypeStruct((b, s, d), F32),
            jax.ShapeDtypeStruct((b, s, d), BF16),
            jax.ShapeDtypeStruct((b, s, LANES), F32),
        ],
        compiler_params=_cparams("parallel", "parallel"),
        name="ret_out",
    )(o, w_out.reshape(heads, dv, d), x, g_ffn, w_router)


def _exclusive_prefix(flags):
    rows, s = flags.shape
    upper = (lax.broadcasted_iota(I32, (LANES, LANES), 0)
             < lax.broadcasted_iota(I32, (LANES, LANES), 1))
    upper = jnp.where(upper, 1.0, 0.0).astype(BF16)
    carry = jnp.zeros((rows, 1), F32)
    pieces = []
    for k in range(s // LANES):
        blk = flags[:, k * LANES:(k + 1) * LANES]
        pieces.append(_dot(blk.astype(BF16), upper) + carry)
        carry = carry + jnp.sum(blk, axis=1, keepdims=True)
    return jnp.concatenate(pieces, axis=1)


def _router_select_kernel(n_exp, cap, l_ref, rank_ref, gate_ref):
    nb = l_ref.shape[0]
    affs = []
    for i in range(nb):
        l = l_ref[i]
        logits = l[0:n_exp] + l[n_exp:2 * n_exp] + l[2 * n_exp:3 * n_exp]
        e = jnp.exp(logits - jnp.max(logits, axis=0, keepdims=True))
        affs.append(e / jnp.sum(e, axis=0, keepdims=True))
    aff = jnp.concatenate(affs, axis=0)
    keys = pltpu.bitcast(aff, I32)

    def refine(i, thr):
        cand = thr | jnp.left_shift(jnp.int32(1), 30 - i)
        cnt = jnp.sum(jnp.where(keys >= cand, 1.0, 0.0), axis=1, keepdims=True)
        return jnp.where(cnt >= cap, cand, thr)

    thr = lax.fori_loop(0, 31, refine, jnp.zeros((nb * n_exp, 1), I32))
    above = keys > thr
    tied = keys == thr
    n_above = jnp.sum(jnp.where(above, 1.0, 0.0), axis=1, keepdims=True)
    tied_rank = _exclusive_prefix(jnp.where(tied, 1.0, 0.0))
    keep = above | (tied & (tied_rank < cap - n_above))
    rank = _exclusive_prefix(jnp.where(keep, 1.0, 0.0))
    rank = jnp.where(keep, rank.astype(I32), -1)
    gate = jnp.where(keep, aff, 0.0)
    for i in range(nb):
        rank_ref[i] = rank[i * n_exp:(i + 1) * n_exp]
        gate_ref[i] = gate[i * n_exp:(i + 1) * n_exp]


def _router_select(lt, n_exp, cap):
    b, rows, s = lt.shape
    nb = _row_tile(b, 4)
    blk = lambda i: (i, 0, 0)
    return pl.pallas_call(
        functools.partial(_router_select_kernel, n_exp, cap),
        grid=(b // nb,),
        in_specs=[pl.BlockSpec((nb, rows, s), blk)],
        out_specs=[
            pl.BlockSpec((nb, n_exp, s), blk),
            pl.BlockSpec((nb, n_exp, s), blk),
        ],
        out_shape=[
            jax.ShapeDtypeStruct((b, n_exp, s), I32),
            jax.ShapeDtypeStruct((b, n_exp, s), F32),
        ],
        compiler_params=_cparams("parallel"),
        name="router_select",
    )(lt)


def _gather_kernel(cap, hn_ref, rank_ref, gate_ref, xe_ref, gs_ref):
    n_exp = rank_ref.shape[1]
    s = hn_ref.shape[1]
    hn = hn_ref[0]
    slot = lax.broadcasted_iota(I32, (cap, s), 0)
    for e in range(n_exp):
        hit = slot == rank_ref[0, e:e + 1, :]
        xe_ref[e, 0] = _dot(jnp.where(hit, 1.0, 0.0).astype(BF16), hn).astype(BF16)
        g = jnp.sum(jnp.where(hit, gate_ref[0, e:e + 1, :], 0.0), axis=1, keepdims=True)
        gs_ref[e, 0] = jnp.broadcast_to(g, (cap, LANES))


def _gather(hn, rank, gate, cap):
    b, s, d = hn.shape
    n_exp = rank.shape[1]
    return pl.pallas_call(
        functools.partial(_gather_kernel, cap),
        grid=(b,),
        in_specs=[
            pl.BlockSpec((1, s, d), lambda i: (i, 0, 0)),
            pl.BlockSpec((1, n_exp, s), lambda i: (i, 0, 0)),
            pl.BlockSpec((1, n_exp, s), lambda i: (i, 0, 0)),
        ],
        out_specs=[
            pl.BlockSpec((n_exp, 1, cap, d), lambda i: (0, i, 0, 0)),
            pl.BlockSpec((n_exp, 1, cap, LANES), lambda i: (0, i, 0, 0)),
        ],
        out_shape=[
            jax.ShapeDtypeStruct((n_exp, b, cap, d), BF16),
            jax.ShapeDtypeStruct((n_exp, b, cap, LANES), F32),
        ],
        compiler_params=_cparams("parallel"),
        name="moe_gather",
    )(hn, rank, gate)


def _ffn_kernel(layer, xe_ref, gs_ref, wg_hbm, wu_hbm, wd_hbm, y_ref,
                wg_s, wu_s, wd_s, stg_g, stg_u, stg_d, sem):
    e = pl.program_id(0)
    m = pl.program_id(1)
    n_exp = pl.num_programs(0)
    nm = pl.num_programs(1)
    n_chunks, d, fc = wg_s.shape[1:]
    cur = e % 2
    par = m % 2

    def chunk_copies(exp, k, buf):
        cols = pl.ds(pl.multiple_of(k * fc, fc), fc)
        return (pltpu.make_async_copy(wg_hbm.at[layer, exp, :, cols], stg_g.at[buf], sem.at[buf, 0]),
                pltpu.make_async_copy(wu_hbm.at[layer, exp, :, cols], stg_u.at[buf], sem.at[buf, 1]),
                pltpu.make_async_copy(wd_hbm.at[layer, exp, cols, :], stg_d.at[buf], sem.at[buf, 2]))

    def convert(buf, slot, k):
        wg_s[slot, k] = stg_g[buf].astype(BF16)
        wu_s[slot, k] = stg_u[buf].astype(BF16)
        wd_s[slot, pl.ds(pl.multiple_of(k * fc, fc), fc), :] = stg_d[buf].astype(BF16)

    first = jnp.logical_and(e == 0, m == 0)

    @pl.when(first)
    def _():
        for k in range(n_chunks):
            for c in chunk_copies(0, k, k % 2):
                c.start()
            for c in chunk_copies(0, k, k % 2):
                c.wait()
            if k < n_chunks - 1:
                convert(k % 2, 0, k)

    in_flight = jnp.where(m == 0, e > 0, e < n_exp - 1)

    @pl.when(in_flight)
    def _():
        for c in chunk_copies(0, 0, 1 - par):
            c.wait()

    @pl.when(e < n_exp - 1)
    def _():
        for c in chunk_copies(e + 1, m, par):
            c.start()

    convert(1 - par, jnp.where(m == 0, cur, 1 - cur), jnp.where(m == 0, nm - 1, m - 1))

    x = xe_ref[0]
    acc = jnp.zeros((x.shape[0], d), F32)
    for k in range(0, n_chunks, 2):
        acts = []
        for kk in (k, k + 1):
            a = _dot(x, wg_s[cur, kk])
            acts.append(a * _sigmoid(a) * _dot(x, wu_s[cur, kk]))
        act = jnp.concatenate(acts, axis=1).astype(BF16)
        acc += _dot(act, wd_s[cur, k * fc:(k + 2) * fc, :])
    y_ref[0] = (acc * gs_ref[0][:, 0:1]).astype(BF16)


def _ffn(layer, xe, gs, wg, wu, wd):
    n_exp, rows, d = xe.shape
    ff = wg.shape[3]
    tm = _row_tile(rows, 1024)
    nm = rows // tm
    assert nm % 2 == 0 and ff % nm == 0 and (ff // nm) % LANES == 0
    fc = ff // nm
    return pl.pallas_call(
        functools.partial(_ffn_kernel, layer),
        grid=(n_exp, nm),
        in_specs=[
            pl.BlockSpec((1, tm, d), lambda e, m: (e, m, 0)),
            pl.BlockSpec((1, tm, LANES), lambda e, m: (e, m, 0)),
            pl.BlockSpec(memory_space=pl.ANY),
            pl.BlockSpec(memory_space=pl.ANY),
            pl.BlockSpec(memory_space=pl.ANY),
        ],
        out_specs=pl.BlockSpec((1, tm, d), lambda e, m: (e, m, 0)),
        out_shape=jax.ShapeDtypeStruct((n_exp, rows, d), BF16),
        scratch_shapes=[
            pltpu.VMEM((2, nm, d, fc), BF16),
            pltpu.VMEM((2, nm, d, fc), BF16),
            pltpu.VMEM((2, ff, d), BF16),
            pltpu.VMEM((2, d, fc), F32),
            pltpu.VMEM((2, d, fc), F32),
            pltpu.VMEM((2, fc, d), F32),
            pltpu.SemaphoreType.DMA((2, 3)),
        ],
        compiler_params=_cparams("arbitrary", "arbitrary"),
        name="moe_ffn",
    )(xe, gs, wg, wu, wd)


def _combine_ple_kernel(final, h_ref, y_ref, rankt_ref, p_ref, gple_ref, wgate_ref, wproj_ref,
                        gfin_ref, o_ref):
    n_exp, _, cap, d = y_ref.shape
    ts = h_ref.shape[1]
    rt = rankt_ref[0]
    slot = lax.broadcasted_iota(I32, (ts, cap), 1)
    onehot = jnp.concatenate(
        [jnp.where(rt[:, e:e + 1] == slot, 1.0, 0.0).astype(BF16) for e in range(n_exp)],
        axis=1)
    y = y_ref[:, 0].reshape(n_exp * cap, d)
    h = h_ref[0] + _dot(onehot, y)
    gate = _sigmoid(_dot(_rms(h, gple_ref[...]).astype(BF16), wgate_ref[...]))
    h = h + gate * _dot(p_ref[...].astype(BF16), wproj_ref[...])
    if final:
        h = _rms(h, gfin_ref[...])
    o_ref[0] = h


def _combine_ple(layer, h, y, rankt, p, g_ple, w_gate, w_proj, g_final, final):
    b, s, d = h.shape
    n_exp, _, cap, _ = y.shape
    ple = p.shape[3]
    ts = _row_tile(s, 512)
    const = lambda i, j: (0, 0)
    return pl.pallas_call(
        functools.partial(_combine_ple_kernel, final),
        grid=(b, s // ts),
        in_specs=[
            pl.BlockSpec((1, ts, d), lambda i, j: (i, j, 0)),
            pl.BlockSpec((n_exp, 1, cap, d), lambda i, j: (0, i, 0, 0)),
            pl.BlockSpec((1, ts, n_exp), lambda i, j: (i, j, 0)),
            pl.BlockSpec((None, None, ts, ple), lambda i, j: (layer, i, j, 0)),
            pl.BlockSpec((1, d), const),
            pl.BlockSpec((d, d), const),
            pl.BlockSpec((ple, d), const),
            pl.BlockSpec((1, d), const),
        ],
        out_specs=pl.BlockSpec((1, ts, d), lambda i, j: (i, j, 0)),
        out_shape=jax.ShapeDtypeStruct((b, s, d), F32),
        compiler_params=_cparams("parallel", "arbitrary"),
        name="combine_ple",
    )(h, y, rankt, p, g_ple, w_gate, w_proj, g_final)


def _split_router_weight(w):
    d, n_exp = w.shape
    hi = w.astype(BF16)
    r1 = w - hi.astype(F32)
    mid = r1.astype(BF16)
    lo = (r1 - mid.astype(F32)).astype(BF16)
    pad = jnp.zeros((d, LANES - 3 * n_exp), BF16)
    return jnp.concatenate([hi, mid, lo, pad], axis=1)


def _moe_ple(layer, h, hn, lraw, p, n_exp, w_gate, w_up, w_down, g_ple, ple_w_gate, ple_w_proj,
             g_final, final):
    b, s, d = h.shape
    cap = EC_CAPACITY_FACTOR * s // n_exp
    lt = jnp.swapaxes(lraw[:, :, :3 * n_exp], 1, 2)
    rank, gate = _router_select(lt, n_exp, cap)
    xe, gs = _gather(hn, rank, gate, cap)
    y = _ffn(layer, xe.reshape(n_exp, b * cap, d), gs.reshape(n_exp, b * cap, LANES),
             w_gate, w_up, w_down)
    rankt = jnp.swapaxes(rank, 1, 2)
    return _combine_ple(layer, h, y.reshape(n_exp, b, cap, d), rankt, p, g_ple,
                        ple_w_gate, ple_w_proj, g_final, final)


def _rope_tables(s, half):
    inv = ROPE_BASE ** (-jnp.arange(half, dtype=F32) / half)
    ang = jnp.arange(s).astype(F32)[:, None] * inv[None, :]
    return jnp.cos(ang), jnp.sin(ang)


def kernel(x, p, norm_mix, norm_ffn, norm_ple, final_norm, conv_w_in, conv_w, conv_b, conv_w_out,
           ret_w_in, ret_log_decay, ret_w_out, router_w, exp_w_gate, exp_w_up, exp_w_down,
           ple_w_proj, ple_w_gate):
    b, s, d = x.shape
    depth = p.shape[0]
    heads = ret_log_decay.shape[2]
    n_exp = router_w.shape[2]
    hq = d
    hv = (ret_w_in.shape[2] - 2 * hq) // 2
    h = x
    for i in range(depth):
        j = i // 2
        g_mix = norm_mix[i][None, :]
        g_ffn = norm_ffn[i][None, :]
        w_router = _split_router_weight(router_w[i])
        if i % 2 == 0:
            h, hn, lraw = _conv_mixer(h, g_mix, conv_w_in[j].astype(BF16), conv_w[j],
                                      conv_b[j][None, :], conv_w_out[j].astype(BF16),
                                      g_ffn, w_router)
        else:
            w_in = ret_w_in[j]
            wq = w_in[:, :hq].astype(BF16)
            wkt = w_in[:, hq:2 * hq].T.astype(BF16)
            wv = w_in[:, 2 * hq:2 * hq + hv].astype(BF16)
            wg = w_in[:, 2 * hq + hv:].astype(BF16)
            cos, sin = _rope_tables(s, hq // heads // 2)
            q, kt, v, sg = _ret_in(h, g_mix, wq, wkt, wv, wg, cos, sin, heads)
            o = _retention(q, kt, v, sg, ret_log_decay[j])
            h, hn, lraw = _ret_out(o, ret_w_out[j].astype(BF16), h, g_ffn, w_router)
        h = _moe_ple(i, h, hn, lraw, p, n_exp, exp_w_gate, exp_w_up, exp_w_down,
                     norm_ple[i][None, :],
                     ple_w_gate[i].astype(BF16), ple_w_proj[i].astype(BF16),
                     final_norm[None, :], i == depth - 1)
    return h
```

```python
import functools

import jax
import jax.numpy as jnp
from jax import lax
from jax.experimental import pallas as pl
from jax.experimental.pallas import tpu as pltpu
from jax.experimental.pallas import tpu_sc as plsc

F32 = jnp.float32
BF16 = jnp.bfloat16
I32 = jnp.int32

NORM_EPS = 1e-6
GN_EPS = 1e-5
ROPE_BASE = 10000.0
EC_CAPACITY_FACTOR = 2
RET_CHUNK = 256
LANES = 128
SUBLANES = 8
SC_LANES = 16
SC_GATHER_WINDOW = 64
VMEM_LIMIT = 56 * 1024 * 1024


def _cparams(*sem):
    return pltpu.CompilerParams(dimension_semantics=sem, vmem_limit_bytes=VMEM_LIMIT)


def _row_tile(n, target):
    t = min(n, target)
    while n % t:
        t //= 2
    return t


def _rms(x, g):
    return x * lax.rsqrt(jnp.mean(x * x, axis=-1, keepdims=True) + NORM_EPS) * g


def _sigmoid(x):
    return 1.0 / (1.0 + jnp.exp(-x))


def _dot(a, b):
    return jnp.dot(a, b, preferred_element_type=F32)


def _pack_bf16_pairs(x):
    w = x.shape[1] // 2
    lo = lax.bitcast_convert_type(x[:, :w].astype(BF16).astype(F32), I32)
    hi = lax.bitcast_convert_type(x[:, w:].astype(BF16).astype(F32), I32)
    return lax.shift_right_logical(lo, jnp.int32(16)) | (hi & jnp.int32(-65536))


def _unpack_bf16_pairs(p):
    lo = lax.bitcast_convert_type(lax.shift_left(p, jnp.int32(16)), F32).astype(BF16)
    hi = lax.bitcast_convert_type(p & jnp.int32(-65536), F32).astype(BF16)
    return jnp.concatenate([lo, hi], axis=1)


def _router_epilogue(h, g_ref, wr_ref, hn_ref, l_ref, rows=slice(None)):
    hn = _rms(h, g_ref[...])
    hi = hn.astype(BF16)
    lo = (hn - hi.astype(F32)).astype(BF16)
    hn_ref[0, rows, :] = _pack_bf16_pairs(hn)
    l_ref[0, rows, :] = _dot(hi, wr_ref[...]) + _dot(lo, wr_ref[...])


def _conv_mixer_kernel(xc_ref, xp_ref, g_ref, win_ref, cw_ref, cb_ref, wout_ref, gffn_ref, wr_ref,
                       o_ref, hn_ref, l_ref, u_ref, bg_ref, edge_ref):
    j = pl.program_id(1)
    nj = pl.num_programs(1) - 1
    ts, d = xc_ref.shape[1], xc_ref.shape[2]
    slot = j % 2

    @pl.when(j == 0)
    def _():
        edge_ref[...] = jnp.zeros_like(edge_ref)

    @pl.when(j < nj)
    def _():
        hn = _rms(xc_ref[0], g_ref[...]).astype(BF16)
        bg_ref[slot] = _dot(hn, win_ref[:, :d]).astype(BF16)
        u_ref[slot] = _dot(hn, win_ref[:, d:2 * d]) * _dot(hn, win_ref[:, 2 * d:])

    @pl.when(j >= 1)
    def _():
        u = u_ref[1 - slot]
        prev_row = edge_ref[0:1, :]
        next_row = jnp.where(j < nj, u_ref[slot][0:1, :], 0.0)
        row = lax.broadcasted_iota(I32, u.shape, 0)
        u_prev = jnp.where(row == 0, prev_row, pltpu.roll(u, 1, axis=0))
        u_next = jnp.where(row == ts - 1, next_row, pltpu.roll(u, ts - 1, axis=0))
        cw = cw_ref[...]
        edge_ref[0:1, :] = u[ts - 1:ts, :]
        sub = ts // 2
        for r in range(0, ts, sub):
            rows = slice(r, r + sub)
            y = (cw[0:1, :] * u_prev[rows] + cw[1:2, :] * u[rows] + cw[2:3, :] * u_next[rows]
                 + cb_ref[...])
            z = (bg_ref[1 - slot, rows, :].astype(F32) * y).astype(BF16)
            h = xp_ref[0, rows, :] + _dot(z, wout_ref[...])
            o_ref[0, rows, :] = h
            _router_epilogue(h, gffn_ref, wr_ref, hn_ref, l_ref, rows)


def _conv_mixer(x, g, w_in, conv_w, conv_b, w_out, g_ffn, w_router):
    b, s, d = x.shape
    ts = _row_tile(s, 512)
    nj = s // ts
    cur = lambda i, j: (i, jnp.minimum(j, nj - 1), 0)
    prev = lambda i, j: (i, jnp.maximum(j - 1, 0), 0)
    const = lambda i, j: (0, 0)
    return pl.pallas_call(
        _conv_mixer_kernel,
        grid=(b, nj + 1),
        in_specs=[
            pl.BlockSpec((1, ts, d), cur),
            pl.BlockSpec((1, ts, d), prev),
            pl.BlockSpec((1, d), const),
            pl.BlockSpec((d, 3 * d), const),
            pl.BlockSpec((3, d), const),
            pl.BlockSpec((1, d), const),
            pl.BlockSpec((d, d), const),
            pl.BlockSpec((1, d), const),
            pl.BlockSpec((d, LANES), const),
        ],
        out_specs=[
            pl.BlockSpec((1, ts, d), prev),
            pl.BlockSpec((1, ts, d // 2), prev),
            pl.BlockSpec((1, ts, LANES), prev),
        ],
        out_shape=[
            jax.ShapeDtypeStruct((b, s, d), F32),
            jax.ShapeDtypeStruct((b, s, d // 2), I32),
            jax.ShapeDtypeStruct((b, s, LANES), F32),
        ],
        scratch_shapes=[
            pltpu.VMEM((2, ts, d), F32),
            pltpu.VMEM((2, ts, d), BF16),
            pltpu.VMEM((SUBLANES, d), F32),
        ],
        compiler_params=_cparams("parallel", "arbitrary"),
        name="conv_mixer",
    )(x, x, g, w_in, conv_w, conv_b, w_out, g_ffn, w_router)


def _ret_in_kernel(x_ref, g_ref, wq_ref, wkt_ref, wv_ref, wg_ref,
                   cos_ref, sin_ref, cost_ref, sint_ref,
                   q_ref, kt_ref, v_ref, sg_ref):
    heads, _, dk = q_ref.shape[1:]
    dv = v_ref.shape[3]
    half = dk // 2
    hn = _rms(x_ref[0], g_ref[...]).astype(BF16)
    q = _dot(hn, wq_ref[...])
    cos = cos_ref[...]
    sin = sin_ref[...]
    for h in range(heads):
        x1 = q[:, h * dk:h * dk + half]
        x2 = q[:, h * dk + half:(h + 1) * dk]
        q_ref[0, h, :, :half] = (x1 * cos - x2 * sin).astype(BF16)
        q_ref[0, h, :, half:] = (x1 * sin + x2 * cos).astype(BF16)
    kt = lax.dot_general(wkt_ref[...], hn, (((1,), (1,)), ((), ())),
                         preferred_element_type=F32)
    cost = cost_ref[...]
    sint = sint_ref[...]
    scale = dk ** -0.5
    for h in range(heads):
        x1 = kt[h * dk:h * dk + half, :]
        x2 = kt[h * dk + half:(h + 1) * dk, :]
        kt_ref[0, h, :half, :] = ((x1 * cost - x2 * sint) * scale).astype(BF16)
        kt_ref[0, h, half:, :] = ((x1 * sint + x2 * cost) * scale).astype(BF16)
    v = _dot(hn, wv_ref[...])
    gate = _dot(hn, wg_ref[...])
    sg = gate * _sigmoid(gate)
    for h in range(heads):
        v_ref[0, h] = v[:, h * dv:(h + 1) * dv].astype(BF16)
        sg_ref[0, h] = sg[:, h * dv:(h + 1) * dv].astype(BF16)


def _ret_in(x, g, wq, wkt, wv, wg, cos, sin, heads):
    b, s, d = x.shape
    hq = wq.shape[1]
    hv = wv.shape[1]
    dk = hq // heads
    dv = hv // heads
    half = cos.shape[1]
    tm = _row_tile(s, 512)
    cost = cos.T
    sint = sin.T
    const = lambda i, j: (0, 0)
    return pl.pallas_call(
        _ret_in_kernel,
        grid=(b, s // tm),
        in_specs=[
            pl.BlockSpec((1, tm, d), lambda i, j: (i, j, 0)),
            pl.BlockSpec((1, d), const),
            pl.BlockSpec((d, hq), const),
            pl.BlockSpec((hq, d), const),
            pl.BlockSpec((d, hv), const),
            pl.BlockSpec((d, hv), const),
            pl.BlockSpec((tm, half), lambda i, j: (j, 0)),
            pl.BlockSpec((tm, half), lambda i, j: (j, 0)),
            pl.BlockSpec((half, tm), lambda i, j: (0, j)),
            pl.BlockSpec((half, tm), lambda i, j: (0, j)),
        ],
        out_specs=[
            pl.BlockSpec((1, heads, tm, dk), lambda i, j: (i, 0, j, 0)),
            pl.BlockSpec((1, heads, dk, tm), lambda i, j: (i, 0, 0, j)),
            pl.BlockSpec((1, heads, tm, dv), lambda i, j: (i, 0, j, 0)),
            pl.BlockSpec((1, heads, tm, dv), lambda i, j: (i, 0, j, 0)),
        ],
        out_shape=[
            jax.ShapeDtypeStruct((b, heads, s, dk), BF16),
            jax.ShapeDtypeStruct((b, heads, dk, s), BF16),
            jax.ShapeDtypeStruct((b, heads, s, dv), BF16),
            jax.ShapeDtypeStruct((b, heads, s, dv), BF16),
        ],
        compiler_params=_cparams("parallel", "parallel"),
        name="ret_in",
    )(x, g, wq, wkt, wv, wg, cos, sin, cost, sint)


def _retention_kernel(chunk, ld_ref, q_ref, kt_ref, v_ref, sg_ref, o_ref, acc_ref, sf_ref, sb_ref):
    h = pl.program_id(1)
    s = q_ref.shape[2]
    L = chunk
    nc = s // L
    lgf = ld_ref[0, h]
    lgb = ld_ref[1, h]
    ii = lax.broadcasted_iota(I32, (L, L), 0)
    jj = lax.broadcasted_iota(I32, (L, L), 1)
    dist = (ii - jj).astype(F32)
    causal = ii >= jj
    decay = jnp.where(causal,
                      jnp.exp(lgf * jnp.where(causal, dist, 0.0)),
                      jnp.exp(lgb * jnp.where(causal, 0.0, -dist)))
    col = lax.broadcasted_iota(I32, (L, 1), 0).astype(F32)
    row = lax.broadcasted_iota(I32, (1, L), 1).astype(F32)
    q_dec_f = jnp.exp(lgf * (col + 1.0)).astype(BF16)
    q_dec_b = jnp.exp(lgb * (L - col)).astype(BF16)
    k_dec_f = jnp.exp(lgf * (L - 1.0 - row)).astype(BF16)
    k_dec_b = jnp.exp(lgb * row).astype(BF16)
    one = jnp.ones((1, 1), F32)
    c_dec_f = jnp.exp(lgf * L * one)
    c_dec_b = jnp.exp(lgb * L * one)

    def finish(c, out):
        sl = slice(c * L, (c + 1) * L)
        mu = jnp.mean(out, axis=-1, keepdims=True)
        cen = out - mu
        var = jnp.mean(cen * cen, axis=-1, keepdims=True)
        o_ref[0, 0, sl, :] = (sg_ref[0, 0, sl, :].astype(F32) * (cen * lax.rsqrt(var + GN_EPS))
                              ).astype(BF16)

    stored = {}

    def contribute(c, part):
        sl = slice(c * L, (c + 1) * L)
        if c not in stored:
            stored[c] = part is not None
            if part is not None:
                acc_ref[sl, :] = part
        elif not stored[c]:
            finish(c, part)
        else:
            finish(c, acc_ref[sl, :] if part is None else acc_ref[sl, :] + part)

    for k in range(nc):
        cf, cb = k, nc - 1 - k
        slf = slice(cf * L, (cf + 1) * L)
        slb = slice(cb * L, (cb + 1) * L)
        qc, ktc, vc = q_ref[0, 0, slf, :], kt_ref[0, 0, :, slf], v_ref[0, 0, slf, :]
        scores = (_dot(qc, ktc) * decay).astype(BF16)
        part_f = _dot(scores, vc)
        if cf > 0:
            part_f += _dot(qc * q_dec_f, sf_ref[...].astype(BF16))
        if cf < nc - 1:
            upd = _dot(ktc * k_dec_f, vc)
            sf_ref[...] = upd if cf == 0 else sf_ref[...] * c_dec_f + upd
        part_b = None
        if cb < nc - 1:
            part_b = _dot(q_ref[0, 0, slb, :] * q_dec_b, sb_ref[...].astype(BF16))
        if cb > 0:
            upd = _dot(kt_ref[0, 0, :, slb] * k_dec_b, v_ref[0, 0, slb, :])
            sb_ref[...] = upd if cb == nc - 1 else sb_ref[...] * c_dec_b + upd
        contribute(cf, part_f)
        contribute(cb, part_b)


def _retention(q, kt, v, sg, log_decay):
    b, heads, s, dk = q.shape
    dv = v.shape[3]
    chunk = min(RET_CHUNK, s)
    return pl.pallas_call(
        functools.partial(_retention_kernel, chunk),
        grid=(b, heads),
        in_specs=[
            pl.BlockSpec(memory_space=pltpu.SMEM),
            pl.BlockSpec((1, 1, s, dk), lambda i, h: (i, h, 0, 0)),
            pl.BlockSpec((1, 1, dk, s), lambda i, h: (i, h, 0, 0)),
            pl.BlockSpec((1, 1, s, dv), lambda i, h: (i, h, 0, 0)),
            pl.BlockSpec((1, 1, s, dv), lambda i, h: (i, h, 0, 0)),
        ],
        out_specs=pl.BlockSpec((1, 1, s, dv), lambda i, h: (i, h, 0, 0)),
        out_shape=jax.ShapeDtypeStruct((b, heads, s, dv), BF16),
        scratch_shapes=[
            pltpu.VMEM((s, dv), F32),
            pltpu.VMEM((dk, dv), F32),
            pltpu.VMEM((dk, dv), F32),
        ],
        compiler_params=_cparams("parallel", "parallel"),
        name="retention",
    )(log_decay, q, kt, v, sg)


def _ret_out_kernel(o_ref, w_ref, x_ref, gffn_ref, wr_ref, h_ref, hn_ref, l_ref):
    heads, ts = o_ref.shape[1], o_ref.shape[2]
    sub = ts // 2
    for r in range(0, ts, sub):
        rows = slice(r, r + sub)
        h = x_ref[0, rows, :]
        for hd in range(heads):
            h = h + _dot(o_ref[0, hd, rows, :], w_ref[hd])
        h_ref[0, rows, :] = h
        _router_epilogue(h, gffn_ref, wr_ref, hn_ref, l_ref, rows)


def _ret_out(o, w_out, x, g_ffn, w_router):
    b, heads, s, dv = o.shape
    d = x.shape[2]
    ts = _row_tile(s, 512)
    tile = lambda i, j: (i, j, 0)
    const = lambda i, j: (0, 0)
    return pl.pallas_call(
        _ret_out_kernel,
        grid=(b, s // ts),
        in_specs=[
            pl.BlockSpec((1, heads, ts, dv), lambda i, j: (i, 0, j, 0)),
            pl.BlockSpec((heads, dv, d), lambda i, j: (0, 0, 0)),
            pl.BlockSpec((1, ts, d), tile),
            pl.BlockSpec((1, d), const),
            pl.BlockSpec((d, LANES), const),
        ],
        out_specs=[
            pl.BlockSpec((1, ts, d), tile),
            pl.BlockSpec((1, ts, d // 2), tile),
            pl.BlockSpec((1, ts, LANES), tile),
        ],
        out_shape=[
            jax.ShapeDtypeStruct((b, s, d), F32),
            jax.ShapeDtypeStruct((b, s, d // 2), I32),
            jax.ShapeDtypeStruct((b, s, LANES), F32),
        ],
        compiler_params=_cparams("parallel", "parallel"),
        name="ret_out",
    )(o, w_out.reshape(heads, dv, d), x, g_ffn, w_router)


def _exclusive_prefix(flags):
    rows, s = flags.shape
    upper = (lax.broadcasted_iota(I32, (LANES, LANES), 0)
             < lax.broadcasted_iota(I32, (LANES, LANES), 1))
    upper = jnp.where(upper, 1.0, 0.0).astype(BF16)
    carry = jnp.zeros((rows, 1), F32)
    pieces = []
    for k in range(s // LANES):
        blk = flags[:, k * LANES:(k + 1) * LANES]
        pieces.append(_dot(blk.astype(BF16), upper) + carry)
        carry = carry + jnp.sum(blk, axis=1, keepdims=True)
    return jnp.concatenate(pieces, axis=1)


def _router_select_kernel(n_exp, cap, l_ref, rank_ref, gate_ref):
    nb = l_ref.shape[0]
    affs = []
    for i in range(nb):
        l = l_ref[i]
        logits = l[0:n_exp] + l[n_exp:2 * n_exp] + l[2 * n_exp:3 * n_exp]
        e = jnp.exp(logits - jnp.max(logits, axis=0, keepdims=True))
        affs.append(e / jnp.sum(e, axis=0, keepdims=True))
    aff = jnp.concatenate(affs, axis=0)
    keys = pltpu.bitcast(aff, I32)

    def refine(i, thr):
        cand = thr | jnp.left_shift(jnp.int32(1), 30 - i)
        cnt = jnp.sum(jnp.where(keys >= cand, 1.0, 0.0), axis=1, keepdims=True)
        return jnp.where(cnt >= cap, cand, thr)

    thr = lax.fori_loop(0, 31, refine, jnp.zeros((nb * n_exp, 1), I32))
    above = keys > thr
    tied = keys == thr
    n_above = jnp.sum(jnp.where(above, 1.0, 0.0), axis=1, keepdims=True)
    tied_rank = _exclusive_prefix(jnp.where(tied, 1.0, 0.0))
    keep = above | (tied & (tied_rank < cap - n_above))
    rank = _exclusive_prefix(jnp.where(keep, 1.0, 0.0))
    rank = jnp.where(keep, rank.astype(I32), -1)
    gate = jnp.where(keep, aff, 0.0)
    for i in range(nb):
        rank_ref[i] = rank[i * n_exp:(i + 1) * n_exp]
        gate_ref[i] = gate[i * n_exp:(i + 1) * n_exp]


def _router_select(lt, n_exp, cap):
    b, rows, s = lt.shape
    nb = _row_tile(b, 4)
    blk = lambda i: (i, 0, 0)
    return pl.pallas_call(
        functools.partial(_router_select_kernel, n_exp, cap),
        grid=(b // nb,),
        in_specs=[pl.BlockSpec((nb, rows, s), blk)],
        out_specs=[
            pl.BlockSpec((nb, n_exp, s), blk),
            pl.BlockSpec((nb, n_exp, s), blk),
        ],
        out_shape=[
            jax.ShapeDtypeStruct((b, n_exp, s), I32),
            jax.ShapeDtypeStruct((b, n_exp, s), F32),
        ],
        compiler_params=_cparams("parallel"),
        name="router_select",
    )(lt)


def _gather_sc_kernel(n_seq, seq_len, n_exp, cap, win, n_cores, pairs_per_worker,
                      table_hbm, rank_hbm, gate_hbm, zeros_hbm, xe_hbm, gs_hbm,
                      rank_v, gate_v, tok_v, rows_v, gs_v, gsem, osem):
    lanes = SC_LANES
    wid = lax.axis_index("s") * n_cores + lax.axis_index("c")
    lane = lax.broadcasted_iota(I32, (lanes,), 0)
    lane0 = jnp.zeros((lanes,), I32)
    pltpu.sync_copy(zeros_hbm, gs_v)
    n_win = cap // win

    @pl.loop(0, pairs_per_worker)
    def _(i):
        pair = wid * pairs_per_worker + i
        b = pair // n_exp
        e = pair % n_exp
        pltpu.sync_copy(rank_hbm.at[pair], rank_v)
        pltpu.sync_copy(gate_hbm.at[pair], gate_v)

        @pl.loop(0, seq_len // lanes)
        def _(j):
            r = rank_v[pl.ds(j * lanes, lanes)]
            keep = r >= 0
            slot = jnp.where(keep, r, 0)
            plsc.store_scatter(tok_v, [slot // win, slot % win], lane + (j * lanes + b * seq_len),
                               mask=keep)
            plsc.store_scatter(gs_v, [slot, lane0], gate_v[pl.ds(j * lanes, lanes)], mask=keep)

        out_base = (e * n_seq + b) * cap

        def fetch(w):
            return pltpu.make_async_copy(table_hbm.at[tok_v.at[w]], rows_v.at[w % 2], gsem.at[w % 2])

        def put(w):
            return pltpu.make_async_copy(rows_v.at[w % 2], xe_hbm.at[pl.ds(out_base + w * win, win)],
                                         osem.at[w % 2])

        fetch(0).start()
        for w in range(n_win):
            if w + 1 < n_win:
                if w >= 1:
                    put(w - 1).wait()
                fetch(w + 1).start()
            fetch(w).wait()
            put(w).start()
        if n_win >= 2:
            put(n_win - 2).wait()
        put(n_win - 1).wait()
        pltpu.sync_copy(gs_v, gs_hbm.at[pl.ds(out_base, cap)])


def _gather_sc(table, rank, gate, n_seq, seq_len, n_exp, cap):
    width = table.shape[1]
    info = plsc.get_sparse_core_info()
    n_workers = info.num_cores * info.num_subcores
    pairs = n_seq * n_exp
    assert info.num_lanes == SC_LANES and pairs % n_workers == 0 and seq_len % SC_LANES == 0
    win = min(cap, SC_GATHER_WINDOW)
    assert cap % win == 0
    mesh = plsc.VectorSubcoreMesh(core_axis_name="c", subcore_axis_name="s")
    call = pl.kernel(
        functools.partial(_gather_sc_kernel, n_seq, seq_len, n_exp, cap, win, info.num_cores,
                          pairs // n_workers),
        mesh=mesh,
        out_type=[jax.ShapeDtypeStruct((n_exp * n_seq * cap, width), I32),
                  jax.ShapeDtypeStruct((n_exp * n_seq * cap, LANES), F32)],
        scratch_types=[
            pltpu.VMEM((seq_len,), I32),
            pltpu.VMEM((seq_len,), F32),
            pltpu.VMEM((cap // win, win), I32),
            pltpu.VMEM((2, win, width), I32),
            pltpu.VMEM((cap, LANES), F32),
            pltpu.SemaphoreType.DMA((2,)),
            pltpu.SemaphoreType.DMA((2,)),
        ],
        compiler_params=pltpu.CompilerParams(needs_layout_passes=False),
        name="moe_gather_sc",
    )
    return call(table, rank, gate, jnp.zeros((cap, LANES), F32))


def _ffn_kernel(layer, xe_ref, gs_ref, wg_hbm, wu_hbm, wd_hbm, y_ref,
                wg_s, wu_s, wd_s, stg_g, stg_u, stg_d, sem):
    e = pl.program_id(0)
    m = pl.program_id(1)
    n_exp = pl.num_programs(0)
    nm = pl.num_programs(1)
    n_chunks, d, fc = wg_s.shape[1:]
    cur = e % 2
    par = m % 2

    def chunk_copies(exp, k, buf):
        cols = pl.ds(pl.multiple_of(k * fc, fc), fc)
        return (pltpu.make_async_copy(wg_hbm.at[layer, exp, :, cols], stg_g.at[buf], sem.at[buf, 0]),
                pltpu.make_async_copy(wu_hbm.at[layer, exp, :, cols], stg_u.at[buf], sem.at[buf, 1]),
                pltpu.make_async_copy(wd_hbm.at[layer, exp, cols, :], stg_d.at[buf], sem.at[buf, 2]))

    def convert(buf, slot, k):
        wg_s[slot, k] = stg_g[buf].astype(BF16)
        wu_s[slot, k] = stg_u[buf].astype(BF16)
        wd_s[slot, pl.ds(pl.multiple_of(k * fc, fc), fc), :] = stg_d[buf].astype(BF16)

    first = jnp.logical_and(e == 0, m == 0)

    @pl.when(first)
    def _():
        for k in range(n_chunks):
            for c in chunk_copies(0, k, k % 2):
                c.start()
            for c in chunk_copies(0, k, k % 2):
                c.wait()
            if k < n_chunks - 1:
                convert(k % 2, 0, k)

    in_flight = jnp.where(m == 0, e > 0, e < n_exp - 1)

    @pl.when(in_flight)
    def _():
        for c in chunk_copies(0, 0, 1 - par):
            c.wait()

    @pl.when(e < n_exp - 1)
    def _():
        for c in chunk_copies(e + 1, m, par):
            c.start()

    convert(1 - par, jnp.where(m == 0, cur, 1 - cur), jnp.where(m == 0, nm - 1, m - 1))

    x = _unpack_bf16_pairs(xe_ref[0])
    acc = jnp.zeros((x.shape[0], d), F32)
    for k in range(0, n_chunks, 2):
        acts = []
        for kk in (k, k + 1):
            a = _dot(x, wg_s[cur, kk])
            acts.append(a * _sigmoid(a) * _dot(x, wu_s[cur, kk]))
        act = jnp.concatenate(acts, axis=1).astype(BF16)
        acc += _dot(act, wd_s[cur, k * fc:(k + 2) * fc, :])
    y_ref[0] = (acc * gs_ref[0][:, 0:1]).astype(BF16)


def _ffn(layer, xe, gs, wg, wu, wd):
    n_exp, rows, half_d = xe.shape
    d = 2 * half_d
    ff = wg.shape[3]
    tm = _row_tile(rows, 1024)
    nm = rows // tm
    assert nm % 2 == 0 and ff % nm == 0 and (ff // nm) % LANES == 0
    fc = ff // nm
    return pl.pallas_call(
        functools.partial(_ffn_kernel, layer),
        grid=(n_exp, nm),
        in_specs=[
            pl.BlockSpec((1, tm, half_d), lambda e, m: (e, m, 0)),
            pl.BlockSpec((1, tm, LANES), lambda e, m: (e, m, 0)),
            pl.BlockSpec(memory_space=pl.ANY),
            pl.BlockSpec(memory_space=pl.ANY),
            pl.BlockSpec(memory_space=pl.ANY),
        ],
        out_specs=pl.BlockSpec((1, tm, d), lambda e, m: (e, m, 0)),
        out_shape=jax.ShapeDtypeStruct((n_exp, rows, d), BF16),
        scratch_shapes=[
            pltpu.VMEM((2, nm, d, fc), BF16),
            pltpu.VMEM((2, nm, d, fc), BF16),
            pltpu.VMEM((2, ff, d), BF16),
            pltpu.VMEM((2, d, fc), F32),
            pltpu.VMEM((2, d, fc), F32),
            pltpu.VMEM((2, fc, d), F32),
            pltpu.SemaphoreType.DMA((2, 3)),
        ],
        compiler_params=_cparams("arbitrary", "arbitrary"),
        name="moe_ffn",
    )(xe, gs, wg, wu, wd)


def _combine_ple_kernel(final, h_ref, y_ref, rankt_ref, p_ref, gple_ref, wgate_ref, wproj_ref,
                        gfin_ref, o_ref):
    n_exp, _, cap, d = y_ref.shape
    ts = h_ref.shape[1]
    rt = rankt_ref[0]
    slot = lax.broadcasted_iota(I32, (ts, cap), 1)
    onehot = jnp.concatenate(
        [jnp.where(rt[:, e:e + 1] == slot, 1.0, 0.0).astype(BF16) for e in range(n_exp)],
        axis=1)
    y = y_ref[:, 0].reshape(n_exp * cap, d)
    h = h_ref[0] + _dot(onehot, y)
    gate = _sigmoid(_dot(_rms(h, gple_ref[...]).astype(BF16), wgate_ref[...]))
    h = h + gate * _dot(p_ref[...].astype(BF16), wproj_ref[...])
    if final:
        h = _rms(h, gfin_ref[...])
    o_ref[0] = h


def _combine_ple(layer, h, y, rankt, p, g_ple, w_gate, w_proj, g_final, final):
    b, s, d = h.shape
    n_exp, _, cap, _ = y.shape
    ple = p.shape[3]
    ts = _row_tile(s, 512)
    const = lambda i, j: (0, 0)
    return pl.pallas_call(
        functools.partial(_combine_ple_kernel, final),
        grid=(b, s // ts),
        in_specs=[
            pl.BlockSpec((1, ts, d), lambda i, j: (i, j, 0)),
            pl.BlockSpec((n_exp, 1, cap, d), lambda i, j: (0, i, 0, 0)),
            pl.BlockSpec((1, ts, n_exp), lambda i, j: (i, j, 0)),
            pl.BlockSpec((None, None, ts, ple), lambda i, j: (layer, i, j, 0)),
            pl.BlockSpec((1, d), const),
            pl.BlockSpec((d, d), const),
            pl.BlockSpec((ple, d), const),
            pl.BlockSpec((1, d), const),
        ],
        out_specs=pl.BlockSpec((1, ts, d), lambda i, j: (i, j, 0)),
        out_shape=jax.ShapeDtypeStruct((b, s, d), F32),
        compiler_params=_cparams("parallel", "arbitrary"),
        name="combine_ple",
    )(h, y, rankt, p, g_ple, w_gate, w_proj, g_final)


def _split_router_weight(w):
    d, n_exp = w.shape
    hi = w.astype(BF16)
    r1 = w - hi.astype(F32)
    mid = r1.astype(BF16)
    lo = (r1 - mid.astype(F32)).astype(BF16)
    pad = jnp.zeros((d, LANES - 3 * n_exp), BF16)
    return jnp.concatenate([hi, mid, lo, pad], axis=1)


def _moe_ple(layer, h, hn, lraw, p, n_exp, w_gate, w_up, w_down, g_ple, ple_w_gate, ple_w_proj,
             g_final, final):
    b, s, d = h.shape
    cap = EC_CAPACITY_FACTOR * s // n_exp
    lt = jnp.swapaxes(lraw[:, :, :3 * n_exp], 1, 2)
    rank, gate = _router_select(lt, n_exp, cap)
    xe, gs = _gather_sc(hn.reshape(b * s, d // 2), rank.reshape(b * n_exp, s),
                        gate.reshape(b * n_exp, s), b, s, n_exp, cap)
    y = _ffn(layer, xe.reshape(n_exp, b * cap, d // 2), gs.reshape(n_exp, b * cap, LANES),
             w_gate, w_up, w_down)
    rankt = jnp.swapaxes(rank, 1, 2)
    return _combine_ple(layer, h, y.reshape(n_exp, b, cap, d), rankt, p, g_ple,
                        ple_w_gate, ple_w_proj, g_final, final)


def _rope_tables(s, half):
    inv = ROPE_BASE ** (-jnp.arange(half, dtype=F32) / half)
    ang = jnp.arange(s).astype(F32)[:, None] * inv[None, :]
    return jnp.cos(ang), jnp.sin(ang)


def kernel(x, p, norm_mix, norm_ffn, norm_ple, final_norm, conv_w_in, conv_w, conv_b, conv_w_out,
           ret_w_in, ret_log_decay, ret_w_out, router_w, exp_w_gate, exp_w_up, exp_w_down,
           ple_w_proj, ple_w_gate):
    b, s, d = x.shape
    depth = p.shape[0]
    heads = ret_log_decay.shape[2]
    n_exp = router_w.shape[2]
    hq = d
    hv = (ret_w_in.shape[2] - 2 * hq) // 2
    h = x
    for i in range(depth):
        j = i // 2
        g_mix = norm_mix[i][None, :]
        g_ffn = norm_ffn[i][None, :]
        w_router = _split_router_weight(router_w[i])
        if i % 2 == 0:
            h, hn, lraw = _conv_mixer(h, g_mix, conv_w_in[j].astype(BF16), conv_w[j],
                                      conv_b[j][None, :], conv_w_out[j].astype(BF16),
                                      g_ffn, w_router)
        else:
            w_in = ret_w_in[j]
            wq = w_in[:, :hq].astype(BF16)
            wkt = w_in[:, hq:2 * hq].T.astype(BF16)
            wv = w_in[:, 2 * hq:2 * hq + hv].astype(BF16)
            wg = w_in[:, 2 * hq + hv:].astype(BF16)
            cos, sin = _rope_tables(s, hq // heads // 2)
            q, kt, v, sg = _ret_in(h, g_mix, wq, wkt, wv, wg, cos, sin, heads)
            o = _retention(q, kt, v, sg, ret_log_decay[j])
            h, hn, lraw = _ret_out(o, ret_w_out[j].astype(BF16), h, g_ffn, w_router)
        h = _moe_ple(i, h, hn, lraw, p, n_exp, exp_w_gate, exp_w_up, exp_w_down,
                     norm_ple[i][None, :],
                     ple_w_gate[i].astype(BF16), ple_w_proj[i].astype(BF16),
                     final_norm[None, :], i == depth - 1)
    return h
```

```python
import functools

import jax
import jax.numpy as jnp
from jax import lax
from jax.experimental import pallas as pl
from jax.experimental.pallas import tpu as pltpu
from jax.experimental.pallas import tpu_sc as plsc

F32 = jnp.float32
BF16 = jnp.bfloat16
I32 = jnp.int32

NORM_EPS = 1e-6
GN_EPS = 1e-5
ROPE_BASE = 10000.0
EC_CAPACITY_FACTOR = 2
RET_CHUNK = 256
LANES = 128
SUBLANES = 8
SC_LANES = 16
SC_GATHER_WINDOW = 64
EXPERT_GROUP_SPLIT = 4
FFN_WEIGHT_CHUNK = 256
VMEM_LIMIT = 56 * 1024 * 1024


def _cparams(*sem):
    return pltpu.CompilerParams(dimension_semantics=sem, vmem_limit_bytes=VMEM_LIMIT)


def _row_tile(n, target):
    t = min(n, target)
    while n % t:
        t //= 2
    return t


def _rms(x, g):
    return x * lax.rsqrt(jnp.mean(x * x, axis=-1, keepdims=True) + NORM_EPS) * g


def _sigmoid(x):
    return 1.0 / (1.0 + jnp.exp(-x))


def _dot(a, b):
    return jnp.dot(a, b, preferred_element_type=F32)


def _pack_bf16_pairs(x):
    w = x.shape[1] // 2
    lo = lax.bitcast_convert_type(x[:, :w].astype(BF16).astype(F32), I32)
    hi = lax.bitcast_convert_type(x[:, w:].astype(BF16).astype(F32), I32)
    return lax.shift_right_logical(lo, jnp.int32(16)) | (hi & jnp.int32(-65536))


def _unpack_bf16_pairs(p):
    lo = lax.bitcast_convert_type(lax.shift_left(p, jnp.int32(16)), F32).astype(BF16)
    hi = lax.bitcast_convert_type(p & jnp.int32(-65536), F32).astype(BF16)
    return jnp.concatenate([lo, hi], axis=1)


def _router_epilogue(h, g_ref, wr_ref, hn_ref, l_ref, rows=slice(None)):
    hn = _rms(h, g_ref[...])
    hi = hn.astype(BF16)
    lo = (hn - hi.astype(F32)).astype(BF16)
    hn_ref[0, rows, :] = _pack_bf16_pairs(hn)
    l_ref[0, rows, :] = _dot(hi, wr_ref[...]) + _dot(lo, wr_ref[...])


def _conv_mixer_kernel(xc_ref, xp_ref, g_ref, win_ref, cw_ref, cb_ref, wout_ref, gffn_ref, wr_ref,
                       o_ref, hn_ref, l_ref, u_ref, bg_ref, edge_ref):
    j = pl.program_id(1)
    nj = pl.num_programs(1) - 1
    ts, d = xc_ref.shape[1], xc_ref.shape[2]
    slot = j % 2

    @pl.when(j == 0)
    def _():
        edge_ref[...] = jnp.zeros_like(edge_ref)

    @pl.when(j < nj)
    def _():
        hn = _rms(xc_ref[0], g_ref[...]).astype(BF16)
        bg_ref[slot] = _dot(hn, win_ref[:, :d]).astype(BF16)
        u_ref[slot] = _dot(hn, win_ref[:, d:2 * d]) * _dot(hn, win_ref[:, 2 * d:])

    @pl.when(j >= 1)
    def _():
        u = u_ref[1 - slot]
        prev_row = edge_ref[0:1, :]
        next_row = jnp.where(j < nj, u_ref[slot][0:1, :], 0.0)
        row = lax.broadcasted_iota(I32, u.shape, 0)
        u_prev = jnp.where(row == 0, prev_row, pltpu.roll(u, 1, axis=0))
        u_next = jnp.where(row == ts - 1, next_row, pltpu.roll(u, ts - 1, axis=0))
        cw = cw_ref[...]
        edge_ref[0:1, :] = u[ts - 1:ts, :]
        sub = ts // 2
        for r in range(0, ts, sub):
            rows = slice(r, r + sub)
            y = (cw[0:1, :] * u_prev[rows] + cw[1:2, :] * u[rows] + cw[2:3, :] * u_next[rows]
                 + cb_ref[...])
            z = (bg_ref[1 - slot, rows, :].astype(F32) * y).astype(BF16)
            h = xp_ref[0, rows, :] + _dot(z, wout_ref[...])
            o_ref[0, rows, :] = h
            _router_epilogue(h, gffn_ref, wr_ref, hn_ref, l_ref, rows)


def _conv_mixer(x, g, w_in, conv_w, conv_b, w_out, g_ffn, w_router):
    b, s, d = x.shape
    ts = _row_tile(s, 512)
    nj = s // ts
    cur = lambda i, j: (i, jnp.minimum(j, nj - 1), 0)
    prev = lambda i, j: (i, jnp.maximum(j - 1, 0), 0)
    const = lambda i, j: (0, 0)
    return pl.pallas_call(
        _conv_mixer_kernel,
        grid=(b, nj + 1),
        in_specs=[
            pl.BlockSpec((1, ts, d), cur),
            pl.BlockSpec((1, ts, d), prev),
            pl.BlockSpec((1, d), const),
            pl.BlockSpec((d, 3 * d), const),
            pl.BlockSpec((3, d), const),
            pl.BlockSpec((1, d), const),
            pl.BlockSpec((d, d), const),
            pl.BlockSpec((1, d), const),
            pl.BlockSpec((d, LANES), const),
        ],
        out_specs=[
            pl.BlockSpec((1, ts, d), prev),
            pl.BlockSpec((1, ts, d // 2), prev),
            pl.BlockSpec((1, ts, LANES), prev),
        ],
        out_shape=[
            jax.ShapeDtypeStruct((b, s, d), F32),
            jax.ShapeDtypeStruct((b, s, d // 2), I32),
            jax.ShapeDtypeStruct((b, s, LANES), F32),
        ],
        scratch_shapes=[
            pltpu.VMEM((2, ts, d), F32),
            pltpu.VMEM((2, ts, d), BF16),
            pltpu.VMEM((SUBLANES, d), F32),
        ],
        compiler_params=_cparams("parallel", "arbitrary"),
        name="conv_mixer",
    )(x, x, g, w_in, conv_w, conv_b, w_out, g_ffn, w_router)


def _ret_in_kernel(x_ref, g_ref, wq_ref, wkt_ref, wv_ref, wg_ref,
                   cos_ref, sin_ref, cost_ref, sint_ref,
                   q_ref, kt_ref, v_ref, sg_ref):
    heads, _, dk = q_ref.shape[1:]
    dv = v_ref.shape[3]
    half = dk // 2
    hn = _rms(x_ref[0], g_ref[...]).astype(BF16)
    q = _dot(hn, wq_ref[...])
    cos = cos_ref[...]
    sin = sin_ref[...]
    for h in range(heads):
        x1 = q[:, h * dk:h * dk + half]
        x2 = q[:, h * dk + half:(h + 1) * dk]
        q_ref[0, h, :, :half] = (x1 * cos - x2 * sin).astype(BF16)
        q_ref[0, h, :, half:] = (x1 * sin + x2 * cos).astype(BF16)
    kt = lax.dot_general(wkt_ref[...], hn, (((1,), (1,)), ((), ())),
                         preferred_element_type=F32)
    cost = cost_ref[...]
    sint = sint_ref[...]
    scale = dk ** -0.5
    for h in range(heads):
        x1 = kt[h * dk:h * dk + half, :]
        x2 = kt[h * dk + half:(h + 1) * dk, :]
        kt_ref[0, h, :half, :] = ((x1 * cost - x2 * sint) * scale).astype(BF16)
        kt_ref[0, h, half:, :] = ((x1 * sint + x2 * cost) * scale).astype(BF16)
    v = _dot(hn, wv_ref[...])
    gate = _dot(hn, wg_ref[...])
    sg = gate * _sigmoid(gate)
    for h in range(heads):
        v_ref[0, h] = v[:, h * dv:(h + 1) * dv].astype(BF16)
        sg_ref[0, h] = sg[:, h * dv:(h + 1) * dv].astype(BF16)


def _ret_in(x, g, wq, wkt, wv, wg, cos, sin, heads):
    b, s, d = x.shape
    hq = wq.shape[1]
    hv = wv.shape[1]
    dk = hq // heads
    dv = hv // heads
    half = cos.shape[1]
    tm = _row_tile(s, 512)
    cost = cos.T
    sint = sin.T
    const = lambda i, j: (0, 0)
    return pl.pallas_call(
        _ret_in_kernel,
        grid=(b, s // tm),
        in_specs=[
            pl.BlockSpec((1, tm, d), lambda i, j: (i, j, 0)),
            pl.BlockSpec((1, d), const),
            pl.BlockSpec((d, hq), const),
            pl.BlockSpec((hq, d), const),
            pl.BlockSpec((d, hv), const),
            pl.BlockSpec((d, hv), const),
            pl.BlockSpec((tm, half), lambda i, j: (j, 0)),
            pl.BlockSpec((tm, half), lambda i, j: (j, 0)),
            pl.BlockSpec((half, tm), lambda i, j: (0, j)),
            pl.BlockSpec((half, tm), lambda i, j: (0, j)),
        ],
        out_specs=[
            pl.BlockSpec((1, heads, tm, dk), lambda i, j: (i, 0, j, 0)),
            pl.BlockSpec((1, heads, dk, tm), lambda i, j: (i, 0, 0, j)),
            pl.BlockSpec((1, heads, tm, dv), lambda i, j: (i, 0, j, 0)),
            pl.BlockSpec((1, heads, tm, dv), lambda i, j: (i, 0, j, 0)),
        ],
        out_shape=[
            jax.ShapeDtypeStruct((b, heads, s, dk), BF16),
            jax.ShapeDtypeStruct((b, heads, dk, s), BF16),
            jax.ShapeDtypeStruct((b, heads, s, dv), BF16),
            jax.ShapeDtypeStruct((b, heads, s, dv), BF16),
        ],
        compiler_params=_cparams("parallel", "parallel"),
        name="ret_in",
    )(x, g, wq, wkt, wv, wg, cos, sin, cost, sint)


def _retention_kernel(chunk, ld_ref, q_ref, kt_ref, v_ref, sg_ref, o_ref, acc_ref, sf_ref, sb_ref):
    h = pl.program_id(1)
    s = q_ref.shape[2]
    L = chunk
    nc = s // L
    lgf = ld_ref[0, h]
    lgb = ld_ref[1, h]
    ii = lax.broadcasted_iota(I32, (L, L), 0)
    jj = lax.broadcasted_iota(I32, (L, L), 1)
    dist = (ii - jj).astype(F32)
    causal = ii >= jj
    decay = jnp.where(causal,
                      jnp.exp(lgf * jnp.where(causal, dist, 0.0)),
                      jnp.exp(lgb * jnp.where(causal, 0.0, -dist)))
    col = lax.broadcasted_iota(I32, (L, 1), 0).astype(F32)
    row = lax.broadcasted_iota(I32, (1, L), 1).astype(F32)
    q_dec_f = jnp.exp(lgf * (col + 1.0)).astype(BF16)
    q_dec_b = jnp.exp(lgb * (L - col)).astype(BF16)
    k_dec_f = jnp.exp(lgf * (L - 1.0 - row)).astype(BF16)
    k_dec_b = jnp.exp(lgb * row).astype(BF16)
    one = jnp.ones((1, 1), F32)
    c_dec_f = jnp.exp(lgf * L * one)
    c_dec_b = jnp.exp(lgb * L * one)

    def finish(c, out):
        sl = slice(c * L, (c + 1) * L)
        mu = jnp.mean(out, axis=-1, keepdims=True)
        cen = out - mu
        var = jnp.mean(cen * cen, axis=-1, keepdims=True)
        o_ref[0, 0, sl, :] = (sg_ref[0, 0, sl, :].astype(F32) * (cen * lax.rsqrt(var + GN_EPS))
                              ).astype(BF16)

    stored = {}

    def contribute(c, part):
        sl = slice(c * L, (c + 1) * L)
        if c not in stored:
            stored[c] = part is not None
            if part is not None:
                acc_ref[sl, :] = part
        elif not stored[c]:
            finish(c, part)
        else:
            finish(c, acc_ref[sl, :] if part is None else acc_ref[sl, :] + part)

    for k in range(nc):
        cf, cb = k, nc - 1 - k
        slf = slice(cf * L, (cf + 1) * L)
        slb = slice(cb * L, (cb + 1) * L)
        qc, ktc, vc = q_ref[0, 0, slf, :], kt_ref[0, 0, :, slf], v_ref[0, 0, slf, :]
        scores = (_dot(qc, ktc) * decay).astype(BF16)
        part_f = _dot(scores, vc)
        if cf > 0:
            part_f += _dot(qc * q_dec_f, sf_ref[...].astype(BF16))
        if cf < nc - 1:
            upd = _dot(ktc * k_dec_f, vc)
            sf_ref[...] = upd if cf == 0 else sf_ref[...] * c_dec_f + upd
        part_b = None
        if cb < nc - 1:
            part_b = _dot(q_ref[0, 0, slb, :] * q_dec_b, sb_ref[...].astype(BF16))
        if cb > 0:
            upd = _dot(kt_ref[0, 0, :, slb] * k_dec_b, v_ref[0, 0, slb, :])
            sb_ref[...] = upd if cb == nc - 1 else sb_ref[...] * c_dec_b + upd
        contribute(cf, part_f)
        contribute(cb, part_b)


def _retention(q, kt, v, sg, log_decay):
    b, heads, s, dk = q.shape
    dv = v.shape[3]
    chunk = min(RET_CHUNK, s)
    return pl.pallas_call(
        functools.partial(_retention_kernel, chunk),
        grid=(b, heads),
        in_specs=[
            pl.BlockSpec(memory_space=pltpu.SMEM),
            pl.BlockSpec((1, 1, s, dk), lambda i, h: (i, h, 0, 0)),
            pl.BlockSpec((1, 1, dk, s), lambda i, h: (i, h, 0, 0)),
            pl.BlockSpec((1, 1, s, dv), lambda i, h: (i, h, 0, 0)),
            pl.BlockSpec((1, 1, s, dv), lambda i, h: (i, h, 0, 0)),
        ],
        out_specs=pl.BlockSpec((1, 1, s, dv), lambda i, h: (i, h, 0, 0)),
        out_shape=jax.ShapeDtypeStruct((b, heads, s, dv), BF16),
        scratch_shapes=[
            pltpu.VMEM((s, dv), F32),
            pltpu.VMEM((dk, dv), F32),
            pltpu.VMEM((dk, dv), F32),
        ],
        compiler_params=_cparams("parallel", "parallel"),
        name="retention",
    )(log_decay, q, kt, v, sg)


def _ret_out_kernel(o_ref, w_ref, x_ref, gffn_ref, wr_ref, h_ref, hn_ref, l_ref):
    heads, ts = o_ref.shape[1], o_ref.shape[2]
    sub = ts // 2
    for r in range(0, ts, sub):
        rows = slice(r, r + sub)
        h = x_ref[0, rows, :]
        for hd in range(heads):
            h = h + _dot(o_ref[0, hd, rows, :], w_ref[hd])
        h_ref[0, rows, :] = h
        _router_epilogue(h, gffn_ref, wr_ref, hn_ref, l_ref, rows)


def _ret_out(o, w_out, x, g_ffn, w_router):
    b, heads, s, dv = o.shape
    d = x.shape[2]
    ts = _row_tile(s, 512)
    tile = lambda i, j: (i, j, 0)
    const = lambda i, j: (0, 0)
    return pl.pallas_call(
        _ret_out_kernel,
        grid=(b, s // ts),
        in_specs=[
            pl.BlockSpec((1, heads, ts, dv), lambda i, j: (i, 0, j, 0)),
            pl.BlockSpec((heads, dv, d), lambda i, j: (0, 0, 0)),
            pl.BlockSpec((1, ts, d), tile),
            pl.BlockSpec((1, d), const),
            pl.BlockSpec((d, LANES), const),
        ],
        out_specs=[
            pl.BlockSpec((1, ts, d), tile),
            pl.BlockSpec((1, ts, d // 2), tile),
            pl.BlockSpec((1, ts, LANES), tile),
        ],
        out_shape=[
            jax.ShapeDtypeStruct((b, s, d), F32),
            jax.ShapeDtypeStruct((b, s, d // 2), I32),
            jax.ShapeDtypeStruct((b, s, LANES), F32),
        ],
        compiler_params=_cparams("parallel", "parallel"),
        name="ret_out",
    )(o, w_out.reshape(heads, dv, d), x, g_ffn, w_router)


def _exclusive_prefix(flags):
    rows, s = flags.shape
    upper = (lax.broadcasted_iota(I32, (LANES, LANES), 0)
             < lax.broadcasted_iota(I32, (LANES, LANES), 1))
    upper = jnp.where(upper, 1.0, 0.0).astype(BF16)
    carry = jnp.zeros((rows, 1), F32)
    pieces = []
    for k in range(s // LANES):
        blk = flags[:, k * LANES:(k + 1) * LANES]
        pieces.append(_dot(blk.astype(BF16), upper) + carry)
        carry = carry + jnp.sum(blk, axis=1, keepdims=True)
    return jnp.concatenate(pieces, axis=1)


def _router_select_kernel(n_exp, cap, l_ref, rank_ref, gate_ref):
    nb = l_ref.shape[0]
    affs = []
    for i in range(nb):
        l = l_ref[i]
        logits = l[0:n_exp] + l[n_exp:2 * n_exp] + l[2 * n_exp:3 * n_exp]
        e = jnp.exp(logits - jnp.max(logits, axis=0, keepdims=True))
        affs.append(e / jnp.sum(e, axis=0, keepdims=True))
    aff = jnp.concatenate(affs, axis=0)
    keys = pltpu.bitcast(aff, I32)

    def refine(i, thr):
        cand = thr | jnp.left_shift(jnp.int32(1), 30 - i)
        cnt = jnp.sum(jnp.where(keys >= cand, 1.0, 0.0), axis=1, keepdims=True)
        return jnp.where(cnt >= cap, cand, thr)

    thr = lax.fori_loop(0, 31, refine, jnp.zeros((nb * n_exp, 1), I32))
    above = keys > thr
    tied = keys == thr
    n_above = jnp.sum(jnp.where(above, 1.0, 0.0), axis=1, keepdims=True)
    tied_rank = _exclusive_prefix(jnp.where(tied, 1.0, 0.0))
    keep = above | (tied & (tied_rank < cap - n_above))
    rank = _exclusive_prefix(jnp.where(keep, 1.0, 0.0))
    rank = jnp.where(keep, rank.astype(I32), -1)
    gate = jnp.where(keep, aff, 0.0)
    for i in range(nb):
        rank_ref[i] = rank[i * n_exp:(i + 1) * n_exp]
        gate_ref[i] = gate[i * n_exp:(i + 1) * n_exp]


def _router_select(lt, n_exp, cap):
    b, rows, s = lt.shape
    nb = _row_tile(b, 4)
    blk = lambda i: (i, 0, 0)
    return pl.pallas_call(
        functools.partial(_router_select_kernel, n_exp, cap),
        grid=(b // nb,),
        in_specs=[pl.BlockSpec((nb, rows, s), blk)],
        out_specs=[
            pl.BlockSpec((nb, n_exp, s), blk),
            pl.BlockSpec((nb, n_exp, s), blk),
        ],
        out_shape=[
            jax.ShapeDtypeStruct((b, n_exp, s), I32),
            jax.ShapeDtypeStruct((b, n_exp, s), F32),
        ],
        compiler_params=_cparams("parallel"),
        name="router_select",
    )(lt)


def _gather_sc_kernel(n_seq, seq_len, n_exp, e0, ne, cap, win, n_cores, pairs_per_worker,
                      table_hbm, rank_hbm, gate_hbm, zeros_hbm, xe_hbm, gs_hbm,
                      rank_v, gate_v, tok_v, rows_v, gs_v, gsem, osem):
    lanes = SC_LANES
    wid = lax.axis_index("s") * n_cores + lax.axis_index("c")
    lane = lax.broadcasted_iota(I32, (lanes,), 0)
    lane0 = jnp.zeros((lanes,), I32)
    pltpu.sync_copy(zeros_hbm, gs_v)
    n_win = cap // win

    @pl.loop(0, pairs_per_worker)
    def _(i):
        pair = wid * pairs_per_worker + i
        b = pair // ne
        e = pair % ne
        row = b * n_exp + e0 + e
        pltpu.sync_copy(rank_hbm.at[row], rank_v)
        pltpu.sync_copy(gate_hbm.at[row], gate_v)

        @pl.loop(0, seq_len // lanes)
        def _(j):
            r = rank_v[pl.ds(j * lanes, lanes)]
            keep = r >= 0
            slot = jnp.where(keep, r, 0)
            plsc.store_scatter(tok_v, [slot // win, slot % win], lane + (j * lanes + b * seq_len),
                               mask=keep)
            plsc.store_scatter(gs_v, [slot, lane0], gate_v[pl.ds(j * lanes, lanes)], mask=keep)

        out_base = (e * n_seq + b) * cap

        def fetch(w):
            return pltpu.make_async_copy(table_hbm.at[tok_v.at[w]], rows_v.at[w % 2], gsem.at[w % 2])

        def put(w):
            return pltpu.make_async_copy(rows_v.at[w % 2], xe_hbm.at[pl.ds(out_base + w * win, win)],
                                         osem.at[w % 2])

        fetch(0).start()
        for w in range(n_win):
            if w + 1 < n_win:
                if w >= 1:
                    put(w - 1).wait()
                fetch(w + 1).start()
            fetch(w).wait()
            put(w).start()
        if n_win >= 2:
            put(n_win - 2).wait()
        put(n_win - 1).wait()
        pltpu.sync_copy(gs_v, gs_hbm.at[pl.ds(out_base, cap)])


def _gather_sc(table, rank, gate, n_seq, seq_len, n_exp, e0, ne, cap):
    width = table.shape[1]
    info = plsc.get_sparse_core_info()
    n_workers = info.num_cores * info.num_subcores
    pairs = n_seq * ne
    assert info.num_lanes == SC_LANES and pairs % n_workers == 0 and seq_len % SC_LANES == 0
    win = min(cap, SC_GATHER_WINDOW)
    assert cap % win == 0
    mesh = plsc.VectorSubcoreMesh(core_axis_name="c", subcore_axis_name="s")
    call = pl.kernel(
        functools.partial(_gather_sc_kernel, n_seq, seq_len, n_exp, e0, ne, cap, win,
                          info.num_cores, pairs // n_workers),
        mesh=mesh,
        out_type=[jax.ShapeDtypeStruct((ne * n_seq * cap, width), I32),
                  jax.ShapeDtypeStruct((ne * n_seq * cap, LANES), F32)],
        scratch_types=[
            pltpu.VMEM((seq_len,), I32),
            pltpu.VMEM((seq_len,), F32),
            pltpu.VMEM((cap // win, win), I32),
            pltpu.VMEM((2, win, width), I32),
            pltpu.VMEM((cap, LANES), F32),
            pltpu.SemaphoreType.DMA((2,)),
            pltpu.SemaphoreType.DMA((2,)),
        ],
        compiler_params=pltpu.CompilerParams(needs_layout_passes=False),
        name="moe_gather_sc",
    )
    return call(table, rank, gate, jnp.zeros((cap, LANES), F32))


def _ffn_kernel(layer, e0, xe_ref, gs_ref, wg_hbm, wu_hbm, wd_hbm, y_ref,
                wg_s, wu_s, wd_s, stg_g, stg_u, stg_d, sem):
    e = pl.program_id(0)
    m = pl.program_id(1)
    n_exp = pl.num_programs(0)
    nm = pl.num_programs(1)
    n_chunks, d, fc = wg_s.shape[1:]
    cur = e % 2
    par = m % 2

    def chunk_copies(exp, k, buf):
        cols = pl.ds(pl.multiple_of(k * fc, fc), fc)
        return (pltpu.make_async_copy(wg_hbm.at[layer, e0 + exp, :, cols], stg_g.at[buf], sem.at[buf, 0]),
                pltpu.make_async_copy(wu_hbm.at[layer, e0 + exp, :, cols], stg_u.at[buf], sem.at[buf, 1]),
                pltpu.make_async_copy(wd_hbm.at[layer, e0 + exp, cols, :], stg_d.at[buf], sem.at[buf, 2]))

    def convert(buf, slot, k):
        wg_s[slot, k] = stg_g[buf].astype(BF16)
        wu_s[slot, k] = stg_u[buf].astype(BF16)
        wd_s[slot, pl.ds(pl.multiple_of(k * fc, fc), fc), :] = stg_d[buf].astype(BF16)

    first = jnp.logical_and(e == 0, m == 0)

    @pl.when(first)
    def _():
        for k in range(n_chunks):
            for c in chunk_copies(0, k, k % 2):
                c.start()
            for c in chunk_copies(0, k, k % 2):
                c.wait()
            if k < n_chunks - 1:
                convert(k % 2, 0, k)

    in_flight = jnp.where(m == 0, e > 0, e < n_exp - 1)

    @pl.when(in_flight)
    def _():
        for c in chunk_copies(0, 0, 1 - par):
            c.wait()

    @pl.when(e < n_exp - 1)
    def _():
        for c in chunk_copies(e + 1, m, par):
            c.start()

    convert(1 - par, jnp.where(m == 0, cur, 1 - cur), jnp.where(m == 0, nm - 1, m - 1))

    x = _unpack_bf16_pairs(xe_ref[0])
    acc = jnp.zeros((x.shape[0], d), F32)
    for k in range(0, n_chunks, 2):
        acts = []
        for kk in (k, k + 1):
            a = _dot(x, wg_s[cur, kk])
            acts.append(a * _sigmoid(a) * _dot(x, wu_s[cur, kk]))
        act = jnp.concatenate(acts, axis=1).astype(BF16)
        acc += _dot(act, wd_s[cur, k * fc:(k + 2) * fc, :])
    y_ref[0] = (acc * gs_ref[0][:, 0:1]).astype(BF16)


def _ffn(layer, e0, xe, gs, wg, wu, wd):
    n_exp, rows, half_d = xe.shape
    d = 2 * half_d
    ff = wg.shape[3]
    fc = min(ff // 2, FFN_WEIGHT_CHUNK)
    nm = ff // fc
    assert ff % fc == 0 and nm % 2 == 0 and rows % nm == 0
    tm = rows // nm
    assert tm % (2 * SUBLANES) == 0
    return pl.pallas_call(
        functools.partial(_ffn_kernel, layer, e0),
        grid=(n_exp, nm),
        in_specs=[
            pl.BlockSpec((1, tm, half_d), lambda e, m: (e, m, 0)),
            pl.BlockSpec((1, tm, LANES), lambda e, m: (e, m, 0)),
            pl.BlockSpec(memory_space=pl.ANY),
            pl.BlockSpec(memory_space=pl.ANY),
            pl.BlockSpec(memory_space=pl.ANY),
        ],
        out_specs=pl.BlockSpec((1, tm, d), lambda e, m: (e, m, 0)),
        out_shape=jax.ShapeDtypeStruct((n_exp, rows, d), BF16),
        scratch_shapes=[
            pltpu.VMEM((2, nm, d, fc), BF16),
            pltpu.VMEM((2, nm, d, fc), BF16),
            pltpu.VMEM((2, ff, d), BF16),
            pltpu.VMEM((2, d, fc), F32),
            pltpu.VMEM((2, d, fc), F32),
            pltpu.VMEM((2, fc, d), F32),
            pltpu.SemaphoreType.DMA((2, 3)),
        ],
        compiler_params=_cparams("arbitrary", "arbitrary"),
        name="moe_ffn",
    )(xe, gs, wg, wu, wd)


def _combine_ple_kernel(final, n_groups, h_ref, *refs):
    y_refs = refs[:n_groups]
    rankt_ref, p_ref, gple_ref, wgate_ref, wproj_ref, gfin_ref, o_ref = refs[n_groups:]
    _, _, cap, d = y_refs[0].shape
    ts = h_ref.shape[1]
    rt = rankt_ref[0]
    slot = lax.broadcasted_iota(I32, (ts, cap), 1)
    h = h_ref[0]
    e0 = 0
    for y_ref in y_refs:
        ne = y_ref.shape[0]
        onehot = jnp.concatenate(
            [jnp.where(rt[:, e:e + 1] == slot, 1.0, 0.0).astype(BF16) for e in range(e0, e0 + ne)],
            axis=1)
        h = h + _dot(onehot, y_ref[:, 0].reshape(ne * cap, d))
        e0 += ne
    gate = _sigmoid(_dot(_rms(h, gple_ref[...]).astype(BF16), wgate_ref[...]))
    h = h + gate * _dot(p_ref[...].astype(BF16), wproj_ref[...])
    if final:
        h = _rms(h, gfin_ref[...])
    o_ref[0] = h


def _combine_ple(layer, b0, h, ys, rankt, p, g_ple, w_gate, w_proj, g_final, final):
    b, s, d = h.shape
    cap = ys[0].shape[2]
    n_exp = sum(y.shape[0] for y in ys)
    ple = p.shape[3]
    ts = _row_tile(s, 512)
    const = lambda i, j: (0, 0)
    return pl.pallas_call(
        functools.partial(_combine_ple_kernel, final, len(ys)),
        grid=(b, s // ts),
        in_specs=[
            pl.BlockSpec((1, ts, d), lambda i, j: (i, j, 0)),
            *[pl.BlockSpec((y.shape[0], 1, cap, d), lambda i, j: (0, i, 0, 0)) for y in ys],
            pl.BlockSpec((1, ts, n_exp), lambda i, j: (i, j, 0)),
            pl.BlockSpec((None, None, ts, ple), lambda i, j: (layer, b0 + i, j, 0)),
            pl.BlockSpec((1, d), const),
            pl.BlockSpec((d, d), const),
            pl.BlockSpec((ple, d), const),
            pl.BlockSpec((1, d), const),
        ],
        out_specs=pl.BlockSpec((1, ts, d), lambda i, j: (i, j, 0)),
        out_shape=jax.ShapeDtypeStruct((b, s, d), F32),
        compiler_params=_cparams("parallel", "arbitrary"),
        name="combine_ple",
    )(h, *ys, rankt, p, g_ple, w_gate, w_proj, g_final)


def _split_router_weight(w):
    d, n_exp = w.shape
    hi = w.astype(BF16)
    r1 = w - hi.astype(F32)
    mid = r1.astype(BF16)
    lo = (r1 - mid.astype(F32)).astype(BF16)
    pad = jnp.zeros((d, LANES - 3 * n_exp), BF16)
    return jnp.concatenate([hi, mid, lo, pad], axis=1)


def _moe_ple(layer, b0, h, hn, lraw, p, n_exp, w_gate, w_up, w_down, g_ple, ple_w_gate, ple_w_proj,
             g_final, final):
    b, s, d = h.shape
    cap = EC_CAPACITY_FACTOR * s // n_exp
    lt = jnp.swapaxes(lraw[:, :, :3 * n_exp], 1, 2)
    rank, gate = _router_select(lt, n_exp, cap)
    split = n_exp - EXPERT_GROUP_SPLIT if 0 < EXPERT_GROUP_SPLIT < n_exp else n_exp
    groups = [(0, split)] + ([(split, n_exp - split)] if split < n_exp else [])
    table = hn.reshape(b * s, d // 2)
    rank2 = rank.reshape(b * n_exp, s)
    gate2 = gate.reshape(b * n_exp, s)
    gathered = [_gather_sc(table, rank2, gate2, b, s, n_exp, e0, ne, cap) for e0, ne in groups]
    ys = []
    for (e0, ne), (xe, gs) in zip(groups, gathered):
        y = _ffn(layer, e0, xe.reshape(ne, b * cap, d // 2), gs.reshape(ne, b * cap, LANES),
                 w_gate, w_up, w_down)
        ys.append(y.reshape(ne, b, cap, d))
    rankt = jnp.swapaxes(rank, 1, 2)
    return _combine_ple(layer, b0, h, ys, rankt, p, g_ple, ple_w_gate, ple_w_proj, g_final, final)


def _rope_tables(s, half):
    inv = ROPE_BASE ** (-jnp.arange(half, dtype=F32) / half)
    ang = jnp.arange(s).astype(F32)[:, None] * inv[None, :]
    return jnp.cos(ang), jnp.sin(ang)


def _forward(x, b0, p, norm_mix, norm_ffn, norm_ple, final_norm, conv_w_in, conv_w, conv_b,
             conv_w_out, ret_w_in, ret_log_decay, ret_w_out, router_w, exp_w_gate, exp_w_up,
             exp_w_down, ple_w_proj, ple_w_gate):
    b, s, d = x.shape
    depth = p.shape[0]
    heads = ret_log_decay.shape[2]
    n_exp = router_w.shape[2]
    hq = d
    hv = (ret_w_in.shape[2] - 2 * hq) // 2
    h = x
    for i in range(depth):
        j = i // 2
        g_mix = norm_mix[i][None, :]
        g_ffn = norm_ffn[i][None, :]
        w_router = _split_router_weight(router_w[i])
        if i % 2 == 0:
            h, hn, lraw = _conv_mixer(h, g_mix, conv_w_in[j].astype(BF16), conv_w[j],
                                      conv_b[j][None, :], conv_w_out[j].astype(BF16),
                                      g_ffn, w_router)
        else:
            w_in = ret_w_in[j]
            wq = w_in[:, :hq].astype(BF16)
            wkt = w_in[:, hq:2 * hq].T.astype(BF16)
            wv = w_in[:, 2 * hq:2 * hq + hv].astype(BF16)
            wg = w_in[:, 2 * hq + hv:].astype(BF16)
            cos, sin = _rope_tables(s, hq // heads // 2)
            q, kt, v, sg = _ret_in(h, g_mix, wq, wkt, wv, wg, cos, sin, heads)
            o = _retention(q, kt, v, sg, ret_log_decay[j])
            h, hn, lraw = _ret_out(o, ret_w_out[j].astype(BF16), h, g_ffn, w_router)
        h = _moe_ple(i, b0, h, hn, lraw, p, n_exp, exp_w_gate, exp_w_up, exp_w_down,
                     norm_ple[i][None, :],
                     ple_w_gate[i].astype(BF16), ple_w_proj[i].astype(BF16),
                     final_norm[None, :], i == depth - 1)
    return h


def kernel(x, p, norm_mix, norm_ffn, norm_ple, final_norm, conv_w_in, conv_w, conv_b, conv_w_out,
           ret_w_in, ret_log_decay, ret_w_out, router_w, exp_w_gate, exp_w_up, exp_w_down,
           ple_w_proj, ple_w_gate):
    return _forward(x, 0, p, norm_mix, norm_ffn, norm_ple, final_norm, conv_w_in, conv_w, conv_b,
                    conv_w_out, ret_w_in, ret_log_decay, ret_w_out, router_w, exp_w_gate, exp_w_up,
                    exp_w_down, ple_w_proj, ple_w_gate)
```

```python
import functools

import jax
import jax.numpy as jnp
from jax import lax
from jax.experimental import pallas as pl
from jax.experimental.pallas import tpu as pltpu
from jax.experimental.pallas import tpu_sc as plsc

F32 = jnp.float32
BF16 = jnp.bfloat16
I32 = jnp.int32

NORM_EPS = 1e-6
GN_EPS = 1e-5
ROPE_BASE = 10000.0
EC_CAPACITY_FACTOR = 2
RET_CHUNK = 256
LANES = 128
SUBLANES = 8
SC_LANES = 16
SC_GATHER_WINDOW = 64
EXPERT_GROUP_SPLIT = 4
FFN_WEIGHT_CHUNK = 256
VMEM_LIMIT = 56 * 1024 * 1024


def _cparams(*sem):
    return pltpu.CompilerParams(dimension_semantics=sem, vmem_limit_bytes=VMEM_LIMIT)


def _row_tile(n, target):
    t = min(n, target)
    while n % t:
        t //= 2
    return t


def _rms(x, g):
    return x * lax.rsqrt(jnp.mean(x * x, axis=-1, keepdims=True) + NORM_EPS) * g


def _sigmoid(x):
    return 1.0 / (1.0 + jnp.exp(-x))


def _dot(a, b):
    return jnp.dot(a, b, preferred_element_type=F32)


def _pack_bf16_pairs(x):
    w = x.shape[1] // 2
    lo = lax.bitcast_convert_type(x[:, :w].astype(BF16).astype(F32), I32)
    hi = lax.bitcast_convert_type(x[:, w:].astype(BF16).astype(F32), I32)
    return lax.shift_right_logical(lo, jnp.int32(16)) | (hi & jnp.int32(-65536))


def _unpack_bf16_pairs(p):
    lo = lax.bitcast_convert_type(lax.shift_left(p, jnp.int32(16)), F32).astype(BF16)
    hi = lax.bitcast_convert_type(p & jnp.int32(-65536), F32).astype(BF16)
    return jnp.concatenate([lo, hi], axis=1)


def _router_epilogue(h, g_ref, wr_ref, hn_ref, l_ref, rows=slice(None)):
    hn = _rms(h, g_ref[...])
    hi = hn.astype(BF16)
    lo = (hn - hi.astype(F32)).astype(BF16)
    hn_ref[0, rows, :] = _pack_bf16_pairs(hn)
    logits = _dot(hi, wr_ref[...]) + _dot(lo, wr_ref[...])
    l_ref[0, :, rows] = logits.T[:l_ref.shape[1], :]


def _conv_mixer_kernel(xc_ref, xp_ref, g_ref, win_ref, cw_ref, cb_ref, wout_ref, gffn_ref, wr_ref,
                       o_ref, hn_ref, l_ref, u_ref, bg_ref, edge_ref):
    j = pl.program_id(1)
    nj = pl.num_programs(1) - 1
    ts, d = xc_ref.shape[1], xc_ref.shape[2]
    slot = j % 2

    @pl.when(j == 0)
    def _():
        edge_ref[...] = jnp.zeros_like(edge_ref)

    @pl.when(j < nj)
    def _():
        hn = _rms(xc_ref[0], g_ref[...]).astype(BF16)
        bg_ref[slot] = _dot(hn, win_ref[:, :d]).astype(BF16)
        u_ref[slot] = _dot(hn, win_ref[:, d:2 * d]) * _dot(hn, win_ref[:, 2 * d:])

    @pl.when(j >= 1)
    def _():
        u = u_ref[1 - slot]
        prev_row = edge_ref[0:1, :]
        next_row = jnp.where(j < nj, u_ref[slot][0:1, :], 0.0)
        row = lax.broadcasted_iota(I32, u.shape, 0)
        u_prev = jnp.where(row == 0, prev_row, pltpu.roll(u, 1, axis=0))
        u_next = jnp.where(row == ts - 1, next_row, pltpu.roll(u, ts - 1, axis=0))
        cw = cw_ref[...]
        edge_ref[0:1, :] = u[ts - 1:ts, :]
        sub = ts // 2
        for r in range(0, ts, sub):
            rows = slice(r, r + sub)
            y = (cw[0:1, :] * u_prev[rows] + cw[1:2, :] * u[rows] + cw[2:3, :] * u_next[rows]
                 + cb_ref[...])
            z = (bg_ref[1 - slot, rows, :].astype(F32) * y).astype(BF16)
            h = xp_ref[0, rows, :] + _dot(z, wout_ref[...])
            o_ref[0, rows, :] = h
            _router_epilogue(h, gffn_ref, wr_ref, hn_ref, l_ref, rows)


def _conv_mixer(x, g, w_in, conv_w, conv_b, w_out, g_ffn, w_router, l_rows):
    b, s, d = x.shape
    ts = _row_tile(s, 512)
    nj = s // ts
    cur = lambda i, j: (i, jnp.minimum(j, nj - 1), 0)
    prev = lambda i, j: (i, jnp.maximum(j - 1, 0), 0)
    const = lambda i, j: (0, 0)
    return pl.pallas_call(
        _conv_mixer_kernel,
        grid=(b, nj + 1),
        in_specs=[
            pl.BlockSpec((1, ts, d), cur),
            pl.BlockSpec((1, ts, d), prev),
            pl.BlockSpec((1, d), const),
            pl.BlockSpec((d, 3 * d), const),
            pl.BlockSpec((3, d), const),
            pl.BlockSpec((1, d), const),
            pl.BlockSpec((d, d), const),
            pl.BlockSpec((1, d), const),
            pl.BlockSpec(w_router.shape, const),
        ],
        out_specs=[
            pl.BlockSpec((1, ts, d), prev),
            pl.BlockSpec((1, ts, d // 2), prev),
            pl.BlockSpec((1, l_rows, ts), lambda i, j: (i, 0, jnp.maximum(j - 1, 0))),
        ],
        out_shape=[
            jax.ShapeDtypeStruct((b, s, d), F32),
            jax.ShapeDtypeStruct((b, s, d // 2), I32),
            jax.ShapeDtypeStruct((b, l_rows, s), F32),
        ],
        scratch_shapes=[
            pltpu.VMEM((2, ts, d), F32),
            pltpu.VMEM((2, ts, d), BF16),
            pltpu.VMEM((SUBLANES, d), F32),
        ],
        compiler_params=_cparams("parallel", "arbitrary"),
        name="conv_mixer",
    )(x, x, g, w_in, conv_w, conv_b, w_out, g_ffn, w_router)


def _ret_in_kernel(x_ref, g_ref, wq_ref, wkt_ref, wv_ref, wg_ref,
                   cos_ref, sin_ref, cost_ref, sint_ref,
                   q_ref, kt_ref, v_ref, sg_ref):
    heads, tm, dk = q_ref.shape[1:]
    dv = v_ref.shape[3]
    half = dk // 2
    scale = dk ** -0.5
    sub = tm // 2
    for r in range(0, tm, sub):
        rows = slice(r, r + sub)
        hn = _rms(x_ref[0, rows, :], g_ref[...]).astype(BF16)
        q = _dot(hn, wq_ref[...])
        cos = cos_ref[rows, :]
        sin = sin_ref[rows, :]
        for h in range(heads):
            x1 = q[:, h * dk:h * dk + half]
            x2 = q[:, h * dk + half:(h + 1) * dk]
            q_ref[0, h, rows, :half] = (x1 * cos - x2 * sin).astype(BF16)
            q_ref[0, h, rows, half:] = (x1 * sin + x2 * cos).astype(BF16)
        kt = lax.dot_general(wkt_ref[...], hn, (((1,), (1,)), ((), ())),
                             preferred_element_type=F32)
        cost = cost_ref[:, rows]
        sint = sint_ref[:, rows]
        for h in range(heads):
            x1 = kt[h * dk:h * dk + half, :]
            x2 = kt[h * dk + half:(h + 1) * dk, :]
            kt_ref[0, h, :half, rows] = ((x1 * cost - x2 * sint) * scale).astype(BF16)
            kt_ref[0, h, half:, rows] = ((x1 * sint + x2 * cost) * scale).astype(BF16)
        v = _dot(hn, wv_ref[...])
        gate = _dot(hn, wg_ref[...])
        sg = gate * _sigmoid(gate)
        for h in range(heads):
            v_ref[0, h, rows, :] = v[:, h * dv:(h + 1) * dv].astype(BF16)
            sg_ref[0, h, rows, :] = sg[:, h * dv:(h + 1) * dv].astype(BF16)


def _ret_in(x, g, wq, wkt, wv, wg, cos, sin, heads):
    b, s, d = x.shape
    hq = wq.shape[1]
    hv = wv.shape[1]
    dk = hq // heads
    dv = hv // heads
    half = cos.shape[1]
    tm = _row_tile(s, 512)
    cost = cos.T
    sint = sin.T
    const = lambda i, j: (0, 0)
    return pl.pallas_call(
        _ret_in_kernel,
        grid=(b, s // tm),
        in_specs=[
            pl.BlockSpec((1, tm, d), lambda i, j: (i, j, 0)),
            pl.BlockSpec((1, d), const),
            pl.BlockSpec((d, hq), const),
            pl.BlockSpec((hq, d), const),
            pl.BlockSpec((d, hv), const),
            pl.BlockSpec((d, hv), const),
            pl.BlockSpec((tm, half), lambda i, j: (j, 0)),
            pl.BlockSpec((tm, half), lambda i, j: (j, 0)),
            pl.BlockSpec((half, tm), lambda i, j: (0, j)),
            pl.BlockSpec((half, tm), lambda i, j: (0, j)),
        ],
        out_specs=[
            pl.BlockSpec((1, heads, tm, dk), lambda i, j: (i, 0, j, 0)),
            pl.BlockSpec((1, heads, dk, tm), lambda i, j: (i, 0, 0, j)),
            pl.BlockSpec((1, heads, tm, dv), lambda i, j: (i, 0, j, 0)),
            pl.BlockSpec((1, heads, tm, dv), lambda i, j: (i, 0, j, 0)),
        ],
        out_shape=[
            jax.ShapeDtypeStruct((b, heads, s, dk), BF16),
            jax.ShapeDtypeStruct((b, heads, dk, s), BF16),
            jax.ShapeDtypeStruct((b, heads, s, dv), BF16),
            jax.ShapeDtypeStruct((b, heads, s, dv), BF16),
        ],
        compiler_params=_cparams("parallel", "parallel"),
        name="ret_in",
    )(x, g, wq, wkt, wv, wg, cos, sin, cost, sint)


def _retention_kernel(chunk, ld_ref, q_ref, kt_ref, v_ref, sg_ref, o_ref, acc_ref, sf_ref, sb_ref):
    h = pl.program_id(1)
    s = q_ref.shape[2]
    L = chunk
    nc = s // L
    lgf = ld_ref[0, h]
    lgb = ld_ref[1, h]
    ii = lax.broadcasted_iota(I32, (L, L), 0)
    jj = lax.broadcasted_iota(I32, (L, L), 1)
    dist = (ii - jj).astype(F32)
    causal = ii >= jj
    decay = jnp.where(causal,
                      jnp.exp(lgf * jnp.where(causal, dist, 0.0)),
                      jnp.exp(lgb * jnp.where(causal, 0.0, -dist)))
    col = lax.broadcasted_iota(I32, (L, 1), 0).astype(F32)
    row = lax.broadcasted_iota(I32, (1, L), 1).astype(F32)
    q_dec_f = jnp.exp(lgf * (col + 1.0)).astype(BF16)
    q_dec_b = jnp.exp(lgb * (L - col)).astype(BF16)
    k_dec_f = jnp.exp(lgf * (L - 1.0 - row)).astype(BF16)
    k_dec_b = jnp.exp(lgb * row).astype(BF16)
    one = jnp.ones((1, 1), F32)
    c_dec_f = jnp.exp(lgf * L * one)
    c_dec_b = jnp.exp(lgb * L * one)

    def finish(c, out):
        sl = slice(c * L, (c + 1) * L)
        mu = jnp.mean(out, axis=-1, keepdims=True)
        cen = out - mu
        var = jnp.mean(cen * cen, axis=-1, keepdims=True)
        o_ref[0, 0, sl, :] = (sg_ref[0, 0, sl, :].astype(F32) * (cen * lax.rsqrt(var + GN_EPS))
                              ).astype(BF16)

    stored = {}

    def contribute(c, part):
        sl = slice(c * L, (c + 1) * L)
        if c not in stored:
            stored[c] = part is not None
            if part is not None:
                acc_ref[sl, :] = part
        elif not stored[c]:
            finish(c, part)
        else:
            finish(c, acc_ref[sl, :] if part is None else acc_ref[sl, :] + part)

    for k in range(nc):
        cf, cb = k, nc - 1 - k
        slf = slice(cf * L, (cf + 1) * L)
        slb = slice(cb * L, (cb + 1) * L)
        qc, ktc, vc = q_ref[0, 0, slf, :], kt_ref[0, 0, :, slf], v_ref[0, 0, slf, :]
        scores = (_dot(qc, ktc) * decay).astype(BF16)
        part_f = _dot(scores, vc)
        if cf > 0:
            part_f += _dot(qc * q_dec_f, sf_ref[...].astype(BF16))
        if cf < nc - 1:
            upd = _dot(ktc * k_dec_f, vc)
            sf_ref[...] = upd if cf == 0 else sf_ref[...] * c_dec_f + upd
        part_b = None
        if cb < nc - 1:
            part_b = _dot(q_ref[0, 0, slb, :] * q_dec_b, sb_ref[...].astype(BF16))
        if cb > 0:
            upd = _dot(kt_ref[0, 0, :, slb] * k_dec_b, v_ref[0, 0, slb, :])
            sb_ref[...] = upd if cb == nc - 1 else sb_ref[...] * c_dec_b + upd
        contribute(cf, part_f)
        contribute(cb, part_b)


def _retention(q, kt, v, sg, log_decay):
    b, heads, s, dk = q.shape
    dv = v.shape[3]
    chunk = min(RET_CHUNK, s)
    return pl.pallas_call(
        functools.partial(_retention_kernel, chunk),
        grid=(b, heads),
        in_specs=[
            pl.BlockSpec(memory_space=pltpu.SMEM),
            pl.BlockSpec((1, 1, s, dk), lambda i, h: (i, h, 0, 0)),
            pl.BlockSpec((1, 1, dk, s), lambda i, h: (i, h, 0, 0)),
            pl.BlockSpec((1, 1, s, dv), lambda i, h: (i, h, 0, 0)),
            pl.BlockSpec((1, 1, s, dv), lambda i, h: (i, h, 0, 0)),
        ],
        out_specs=pl.BlockSpec((1, 1, s, dv), lambda i, h: (i, h, 0, 0)),
        out_shape=jax.ShapeDtypeStruct((b, heads, s, dv), BF16),
        scratch_shapes=[
            pltpu.VMEM((s, dv), F32),
            pltpu.VMEM((dk, dv), F32),
            pltpu.VMEM((dk, dv), F32),
        ],
        compiler_params=_cparams("parallel", "parallel"),
        name="retention",
    )(log_decay, q, kt, v, sg)


def _ret_out_kernel(o_ref, w_ref, x_ref, gffn_ref, wr_ref, h_ref, hn_ref, l_ref):
    heads, ts = o_ref.shape[1], o_ref.shape[2]
    sub = ts // 2
    for r in range(0, ts, sub):
        rows = slice(r, r + sub)
        h = x_ref[0, rows, :]
        for hd in range(heads):
            h = h + _dot(o_ref[0, hd, rows, :], w_ref[hd])
        h_ref[0, rows, :] = h
        _router_epilogue(h, gffn_ref, wr_ref, hn_ref, l_ref, rows)


def _ret_out(o, w_out, x, g_ffn, w_router, l_rows):
    b, heads, s, dv = o.shape
    d = x.shape[2]
    ts = _row_tile(s, 512)
    tile = lambda i, j: (i, j, 0)
    const = lambda i, j: (0, 0)
    return pl.pallas_call(
        _ret_out_kernel,
        grid=(b, s // ts),
        in_specs=[
            pl.BlockSpec((1, heads, ts, dv), lambda i, j: (i, 0, j, 0)),
            pl.BlockSpec((heads, dv, d), lambda i, j: (0, 0, 0)),
            pl.BlockSpec((1, ts, d), tile),
            pl.BlockSpec((1, d), const),
            pl.BlockSpec(w_router.shape, const),
        ],
        out_specs=[
            pl.BlockSpec((1, ts, d), tile),
            pl.BlockSpec((1, ts, d // 2), tile),
            pl.BlockSpec((1, l_rows, ts), lambda i, j: (i, 0, j)),
        ],
        out_shape=[
            jax.ShapeDtypeStruct((b, s, d), F32),
            jax.ShapeDtypeStruct((b, s, d // 2), I32),
            jax.ShapeDtypeStruct((b, l_rows, s), F32),
        ],
        compiler_params=_cparams("parallel", "parallel"),
        name="ret_out",
    )(o, w_out.reshape(heads, dv, d), x, g_ffn, w_router)


def _exclusive_prefix(flags):
    rows, s = flags.shape
    upper = (lax.broadcasted_iota(I32, (LANES, LANES), 0)
             < lax.broadcasted_iota(I32, (LANES, LANES), 1))
    upper = jnp.where(upper, 1.0, 0.0).astype(BF16)
    carry = jnp.zeros((rows, 1), F32)
    pieces = []
    for k in range(s // LANES):
        blk = flags[:, k * LANES:(k + 1) * LANES]
        pieces.append(_dot(blk.astype(BF16), upper) + carry)
        carry = carry + jnp.sum(blk, axis=1, keepdims=True)
    return jnp.concatenate(pieces, axis=1)


def _router_select_kernel(n_exp, cap, l_ref, rank_ref, gate_ref):
    nb = l_ref.shape[0]
    affs = []
    for i in range(nb):
        l = l_ref[i]
        logits = l[0:n_exp] + l[n_exp:2 * n_exp] + l[2 * n_exp:3 * n_exp]
        e = jnp.exp(logits - jnp.max(logits, axis=0, keepdims=True))
        affs.append(e / jnp.sum(e, axis=0, keepdims=True))
    aff = jnp.concatenate(affs, axis=0)
    keys = pltpu.bitcast(aff, I32)

    def refine(i, thr):
        cand = thr | jnp.left_shift(jnp.int32(1), 30 - i)
        cnt = jnp.sum(jnp.where(keys >= cand, 1.0, 0.0), axis=1, keepdims=True)
        return jnp.where(cnt >= cap, cand, thr)

    thr = lax.fori_loop(0, 31, refine, jnp.zeros((nb * n_exp, 1), I32))
    above = keys > thr
    tied = keys == thr
    n_above = jnp.sum(jnp.where(above, 1.0, 0.0), axis=1, keepdims=True)
    tied_rank = _exclusive_prefix(jnp.where(tied, 1.0, 0.0))
    keep = above | (tied & (tied_rank < cap - n_above))
    rank = _exclusive_prefix(jnp.where(keep, 1.0, 0.0))
    rank = jnp.where(keep, rank.astype(I32), -1)
    gate = jnp.where(keep, aff, 0.0)
    for i in range(nb):
        rank_ref[i] = rank[i * n_exp:(i + 1) * n_exp]
        gate_ref[i] = gate[i * n_exp:(i + 1) * n_exp]


def _router_select(lt, n_exp, cap):
    b, rows, s = lt.shape
    nb = _row_tile(b, 4)
    blk = lambda i: (i, 0, 0)
    return pl.pallas_call(
        functools.partial(_router_select_kernel, n_exp, cap),
        grid=(b // nb,),
        in_specs=[pl.BlockSpec((nb, rows, s), blk)],
        out_specs=[
            pl.BlockSpec((nb, n_exp, s), blk),
            pl.BlockSpec((nb, n_exp, s), blk),
        ],
        out_shape=[
            jax.ShapeDtypeStruct((b, n_exp, s), I32),
            jax.ShapeDtypeStruct((b, n_exp, s), F32),
        ],
        compiler_params=_cparams("parallel"),
        name="router_select",
    )(lt)


def _gather_sc_kernel(n_seq, seq_len, n_exp, e0, ne, cap, win, n_cores, pairs_per_worker,
                      table_hbm, rank_hbm, gate_hbm, zeros_hbm, xe_hbm, gs_hbm,
                      rank_v, gate_v, tok_v, rows_v, gs_v, gsem, osem):
    lanes = SC_LANES
    wid = lax.axis_index("s") * n_cores + lax.axis_index("c")
    lane = lax.broadcasted_iota(I32, (lanes,), 0)
    lane0 = jnp.zeros((lanes,), I32)
    pltpu.sync_copy(zeros_hbm, gs_v)
    n_win = cap // win

    @pl.loop(0, pairs_per_worker)
    def _(i):
        pair = wid * pairs_per_worker + i
        b = pair // ne
        e = pair % ne
        row = b * n_exp + e0 + e
        pltpu.sync_copy(rank_hbm.at[row], rank_v)
        pltpu.sync_copy(gate_hbm.at[row], gate_v)

        @pl.loop(0, seq_len // lanes)
        def _(j):
            r = rank_v[pl.ds(j * lanes, lanes)]
            keep = r >= 0
            slot = jnp.where(keep, r, 0)
            plsc.store_scatter(tok_v, [slot // win, slot % win], lane + (j * lanes + b * seq_len),
                               mask=keep)
            plsc.store_scatter(gs_v, [slot, lane0], gate_v[pl.ds(j * lanes, lanes)], mask=keep)

        out_base = (e * n_seq + b) * cap

        def fetch(w):
            return pltpu.make_async_copy(table_hbm.at[tok_v.at[w]], rows_v.at[w % 2], gsem.at[w % 2])

        def put(w):
            return pltpu.make_async_copy(rows_v.at[w % 2], xe_hbm.at[pl.ds(out_base + w * win, win)],
                                         osem.at[w % 2])

        fetch(0).start()
        for w in range(n_win):
            if w + 1 < n_win:
                if w >= 1:
                    put(w - 1).wait()
                fetch(w + 1).start()
            fetch(w).wait()
            put(w).start()
        if n_win >= 2:
            put(n_win - 2).wait()
        put(n_win - 1).wait()
        pltpu.sync_copy(gs_v, gs_hbm.at[pl.ds(out_base, cap)])


def _gather_sc(table, rank, gate, n_seq, seq_len, n_exp, e0, ne, cap):
    width = table.shape[1]
    info = plsc.get_sparse_core_info()
    n_workers = info.num_cores * info.num_subcores
    pairs = n_seq * ne
    assert info.num_lanes == SC_LANES and pairs % n_workers == 0 and seq_len % SC_LANES == 0
    win = min(cap, SC_GATHER_WINDOW)
    assert cap % win == 0
    mesh = plsc.VectorSubcoreMesh(core_axis_name="c", subcore_axis_name="s")
    call = pl.kernel(
        functools.partial(_gather_sc_kernel, n_seq, seq_len, n_exp, e0, ne, cap, win,
                          info.num_cores, pairs // n_workers),
        mesh=mesh,
        out_type=[jax.ShapeDtypeStruct((ne * n_seq * cap, width), I32),
                  jax.ShapeDtypeStruct((ne * n_seq * cap, LANES), F32)],
        scratch_types=[
            pltpu.VMEM((seq_len,), I32),
            pltpu.VMEM((seq_len,), F32),
            pltpu.VMEM((cap // win, win), I32),
            pltpu.VMEM((2, win, width), I32),
            pltpu.VMEM((cap, LANES), F32),
            pltpu.SemaphoreType.DMA((2,)),
            pltpu.SemaphoreType.DMA((2,)),
        ],
        compiler_params=pltpu.CompilerParams(needs_layout_passes=False),
        name="moe_gather_sc",
    )
    return call(table, rank, gate, jnp.zeros((cap, LANES), F32))


def _ffn_kernel(layer, e0, xe_ref, gs_ref, wg_hbm, wu_hbm, wd_hbm, y_ref,
                wg_s, wu_s, wd_s, stg_g, stg_u, stg_d, sem):
    e = pl.program_id(0)
    m = pl.program_id(1)
    n_exp = pl.num_programs(0)
    nm = pl.num_programs(1)
    n_chunks, d, fc = wg_s.shape[1:]
    cur = e % 2
    par = m % 2

    def chunk_copies(exp, k, buf):
        cols = pl.ds(pl.multiple_of(k * fc, fc), fc)
        return (pltpu.make_async_copy(wg_hbm.at[layer, e0 + exp, :, cols], stg_g.at[buf], sem.at[buf, 0]),
                pltpu.make_async_copy(wu_hbm.at[layer, e0 + exp, :, cols], stg_u.at[buf], sem.at[buf, 1]),
                pltpu.make_async_copy(wd_hbm.at[layer, e0 + exp, cols, :], stg_d.at[buf], sem.at[buf, 2]))

    def convert(buf, slot, k):
        wg_s[slot, k] = stg_g[buf].astype(BF16)
        wu_s[slot, k] = stg_u[buf].astype(BF16)
        wd_s[slot, pl.ds(pl.multiple_of(k * fc, fc), fc), :] = stg_d[buf].astype(BF16)

    first = jnp.logical_and(e == 0, m == 0)

    @pl.when(first)
    def _():
        for c in chunk_copies(0, 0, 0):
            c.start()
        for k in range(n_chunks):
            if k + 1 < n_chunks:
                for c in chunk_copies(0, k + 1, (k + 1) % 2):
                    c.start()
            for c in chunk_copies(0, k, k % 2):
                c.wait()
            if k < n_chunks - 1:
                convert(k % 2, 0, k)

    in_flight = jnp.where(m == 0, e > 0, e < n_exp - 1)

    @pl.when(in_flight)
    def _():
        for c in chunk_copies(0, 0, 1 - par):
            c.wait()

    @pl.when(e < n_exp - 1)
    def _():
        for c in chunk_copies(e + 1, m, par):
            c.start()

    convert(1 - par, jnp.where(m == 0, cur, 1 - cur), jnp.where(m == 0, nm - 1, m - 1))

    x = _unpack_bf16_pairs(xe_ref[0])
    acc = jnp.zeros((x.shape[0], d), F32)
    for k in range(0, n_chunks, 2):
        acts = []
        for kk in (k, k + 1):
            a = _dot(x, wg_s[cur, kk])
            acts.append(a * _sigmoid(a) * _dot(x, wu_s[cur, kk]))
        act = jnp.concatenate(acts, axis=1).astype(BF16)
        acc += _dot(act, wd_s[cur, k * fc:(k + 2) * fc, :])
    y_ref[0] = (acc * gs_ref[0][:, 0:1]).astype(BF16)


def _ffn(layer, e0, xe, gs, wg, wu, wd):
    n_exp, rows, half_d = xe.shape
    d = 2 * half_d
    ff = wg.shape[3]
    fc = min(ff // 2, FFN_WEIGHT_CHUNK)
    nm = ff // fc
    assert ff % fc == 0 and nm % 2 == 0 and rows % nm == 0
    tm = rows // nm
    assert tm % (2 * SUBLANES) == 0
    return pl.pallas_call(
        functools.partial(_ffn_kernel, layer, e0),
        grid=(n_exp, nm),
        in_specs=[
            pl.BlockSpec((1, tm, half_d), lambda e, m: (e, m, 0)),
            pl.BlockSpec((1, tm, LANES), lambda e, m: (e, m, 0)),
            pl.BlockSpec(memory_space=pl.ANY),
            pl.BlockSpec(memory_space=pl.ANY),
            pl.BlockSpec(memory_space=pl.ANY),
        ],
        out_specs=pl.BlockSpec((1, tm, d), lambda e, m: (e, m, 0)),
        out_shape=jax.ShapeDtypeStruct((n_exp, rows, d), BF16),
        scratch_shapes=[
            pltpu.VMEM((2, nm, d, fc), BF16),
            pltpu.VMEM((2, nm, d, fc), BF16),
            pltpu.VMEM((2, ff, d), BF16),
            pltpu.VMEM((2, d, fc), F32),
            pltpu.VMEM((2, d, fc), F32),
            pltpu.VMEM((2, fc, d), F32),
            pltpu.SemaphoreType.DMA((2, 3)),
        ],
        compiler_params=_cparams("arbitrary", "arbitrary"),
        name="moe_ffn",
    )(xe, gs, wg, wu, wd)


def _combine_ple_kernel(final, n_groups, h_ref, *refs):
    y_refs = refs[:n_groups]
    rankt_ref, p_ref, gple_ref, wgate_ref, wproj_ref, gfin_ref, o_ref = refs[n_groups:]
    _, _, cap, d = y_refs[0].shape
    ts = h_ref.shape[1]
    rt = rankt_ref[0]
    slot = lax.broadcasted_iota(I32, (ts, cap), 1)
    h = h_ref[0]
    e0 = 0
    for y_ref in y_refs:
        ne = y_ref.shape[0]
        onehot = jnp.concatenate(
            [jnp.where(rt[:, e:e + 1] == slot, 1.0, 0.0).astype(BF16) for e in range(e0, e0 + ne)],
            axis=1)
        h = h + _dot(onehot, y_ref[:, 0].reshape(ne * cap, d))
        e0 += ne
    gate = _sigmoid(_dot(_rms(h, gple_ref[...]).astype(BF16), wgate_ref[...]))
    h = h + gate * _dot(p_ref[...].astype(BF16), wproj_ref[...])
    if final:
        h = _rms(h, gfin_ref[...])
    o_ref[0] = h


def _combine_ple(layer, b0, h, ys, rankt, p, g_ple, w_gate, w_proj, g_final, final):
    b, s, d = h.shape
    cap = ys[0].shape[2]
    n_exp = sum(y.shape[0] for y in ys)
    ple = p.shape[3]
    ts = _row_tile(s, 512)
    const = lambda i, j: (0, 0)
    return pl.pallas_call(
        functools.partial(_combine_ple_kernel, final, len(ys)),
        grid=(b, s // ts),
        in_specs=[
            pl.BlockSpec((1, ts, d), lambda i, j: (i, j, 0)),
            *[pl.BlockSpec((y.shape[0], 1, cap, d), lambda i, j: (0, i, 0, 0)) for y in ys],
            pl.BlockSpec((1, ts, n_exp), lambda i, j: (i, j, 0)),
            pl.BlockSpec((None, None, ts, ple), lambda i, j: (layer, b0 + i, j, 0)),
            pl.BlockSpec((1, d), const),
            pl.BlockSpec((d, d), const),
            pl.BlockSpec((ple, d), const),
            pl.BlockSpec((1, d), const),
        ],
        out_specs=pl.BlockSpec((1, ts, d), lambda i, j: (i, j, 0)),
        out_shape=jax.ShapeDtypeStruct((b, s, d), F32),
        compiler_params=_cparams("parallel", "arbitrary"),
        name="combine_ple",
    )(h, *ys, rankt, p, g_ple, w_gate, w_proj, g_final)


def _split_router_weight(w):
    d, n_exp = w.shape
    hi = w.astype(BF16)
    r1 = w - hi.astype(F32)
    mid = r1.astype(BF16)
    lo = (r1 - mid.astype(F32)).astype(BF16)
    pad = jnp.zeros((d, LANES - 3 * n_exp), BF16)
    return jnp.concatenate([hi, mid, lo, pad], axis=1)


def _moe_ple(layer, b0, h, hn, lt, p, n_exp, w_gate, w_up, w_down, g_ple, ple_w_gate, ple_w_proj,
             g_final, final):
    b, s, d = h.shape
    cap = EC_CAPACITY_FACTOR * s // n_exp
    rank, gate = _router_select(lt, n_exp, cap)
    split = n_exp - EXPERT_GROUP_SPLIT if 0 < EXPERT_GROUP_SPLIT < n_exp else n_exp
    groups = [(0, split)] + ([(split, n_exp - split)] if split < n_exp else [])
    table = hn.reshape(b * s, d // 2)
    rank2 = rank.reshape(b * n_exp, s)
    gate2 = gate.reshape(b * n_exp, s)
    gathered = [_gather_sc(table, rank2, gate2, b, s, n_exp, e0, ne, cap) for e0, ne in groups]
    ys = []
    for (e0, ne), (xe, gs) in zip(groups, gathered):
        y = _ffn(layer, e0, xe.reshape(ne, b * cap, d // 2), gs.reshape(ne, b * cap, LANES),
                 w_gate, w_up, w_down)
        ys.append(y.reshape(ne, b, cap, d))
    rankt = jnp.swapaxes(rank, 1, 2)
    return _combine_ple(layer, b0, h, ys, rankt, p, g_ple, ple_w_gate, ple_w_proj, g_final, final)


def _rope_tables(s, half):
    inv = ROPE_BASE ** (-jnp.arange(half, dtype=F32) / half)
    ang = jnp.arange(s).astype(F32)[:, None] * inv[None, :]
    return jnp.cos(ang), jnp.sin(ang)


def _forward(x, b0, p, norm_mix, norm_ffn, norm_ple, final_norm, conv_w_in, conv_w, conv_b,
             conv_w_out, ret_w_in, ret_log_decay, ret_w_out, router_w, exp_w_gate, exp_w_up,
             exp_w_down, ple_w_proj, ple_w_gate):
    b, s, d = x.shape
    depth = p.shape[0]
    heads = ret_log_decay.shape[2]
    n_exp = router_w.shape[2]
    hq = d
    hv = (ret_w_in.shape[2] - 2 * hq) // 2
    h = x
    for i in range(depth):
        j = i // 2
        g_mix = norm_mix[i][None, :]
        g_ffn = norm_ffn[i][None, :]
        w_router = _split_router_weight(router_w[i])
        if i % 2 == 0:
            h, hn, lt = _conv_mixer(h, g_mix, conv_w_in[j].astype(BF16), conv_w[j],
                                      conv_b[j][None, :], conv_w_out[j].astype(BF16),
                                      g_ffn, w_router, 3 * n_exp)
        else:
            w_in = ret_w_in[j]
            wq = w_in[:, :hq].astype(BF16)
            wkt = w_in[:, hq:2 * hq].T.astype(BF16)
            wv = w_in[:, 2 * hq:2 * hq + hv].astype(BF16)
            wg = w_in[:, 2 * hq + hv:].astype(BF16)
            cos, sin = _rope_tables(s, hq // heads // 2)
            q, kt, v, sg = _ret_in(h, g_mix, wq, wkt, wv, wg, cos, sin, heads)
            o = _retention(q, kt, v, sg, ret_log_decay[j])
            h, hn, lt = _ret_out(o, ret_w_out[j].astype(BF16), h, g_ffn, w_router, 3 * n_exp)
        h = _moe_ple(i, b0, h, hn, lt, p, n_exp, exp_w_gate, exp_w_up, exp_w_down,
                     norm_ple[i][None, :],
                     ple_w_gate[i].astype(BF16), ple_w_proj[i].astype(BF16),
                     final_norm[None, :], i == depth - 1)
    return h


def kernel(x, p, norm_mix, norm_ffn, norm_ple, final_norm, conv_w_in, conv_w, conv_b, conv_w_out,
           ret_w_in, ret_log_decay, ret_w_out, router_w, exp_w_gate, exp_w_up, exp_w_down,
           ple_w_proj, ple_w_gate):
    return _forward(x, 0, p, norm_mix, norm_ffn, norm_ple, final_norm, conv_w_in, conv_w, conv_b,
                    conv_w_out, ret_w_in, ret_log_decay, ret_w_out, router_w, exp_w_gate, exp_w_up,
                    exp_w_down, ple_w_proj, ple_w_gate)
```

```python
import functools

import jax
import jax.numpy as jnp
from jax import lax
from jax.experimental import pallas as pl
from jax.experimental.pallas import tpu as pltpu
from jax.experimental.pallas import tpu_sc as plsc

F32 = jnp.float32
BF16 = jnp.bfloat16
I32 = jnp.int32

NORM_EPS = 1e-6
GN_EPS = 1e-5
ROPE_BASE = 10000.0
EC_CAPACITY_FACTOR = 2
RET_CHUNK = 256
LANES = 128
SUBLANES = 8
SC_LANES = 16
SC_GATHER_WINDOW = 64
EXPERT_GROUP_SPLIT = 4
FFN_WEIGHT_CHUNK = 256
VMEM_LIMIT = 56 * 1024 * 1024


def _cparams(*sem):
    return pltpu.CompilerParams(dimension_semantics=sem, vmem_limit_bytes=VMEM_LIMIT)


def _row_tile(n, target):
    t = min(n, target)
    while n % t:
        t //= 2
    return t


def _rms(x, g):
    return x * lax.rsqrt(jnp.mean(x * x, axis=-1, keepdims=True) + NORM_EPS) * g


def _sigmoid(x):
    return 1.0 / (1.0 + jnp.exp(-x))


def _dot(a, b):
    return jnp.dot(a, b, preferred_element_type=F32)


def _pack_bf16_pairs(x):
    w = x.shape[1] // 2
    lo = lax.bitcast_convert_type(x[:, :w].astype(BF16).astype(F32), I32)
    hi = lax.bitcast_convert_type(x[:, w:].astype(BF16).astype(F32), I32)
    return lax.shift_right_logical(lo, jnp.int32(16)) | (hi & jnp.int32(-65536))


def _unpack_bf16_pairs(p):
    lo = lax.bitcast_convert_type(lax.shift_left(p, jnp.int32(16)), F32).astype(BF16)
    hi = lax.bitcast_convert_type(p & jnp.int32(-65536), F32).astype(BF16)
    return jnp.concatenate([lo, hi], axis=1)


def _router_epilogue(h, g_ref, wr_ref, hn_ref, l_ref, rows=slice(None)):
    hn = _rms(h, g_ref[...])
    hi = hn.astype(BF16)
    lo = (hn - hi.astype(F32)).astype(BF16)
    hn_ref[0, rows, :] = _pack_bf16_pairs(hn)
    logits = _dot(hi, wr_ref[...]) + _dot(lo, wr_ref[...])
    l_ref[0, :, rows] = logits.T[:l_ref.shape[1], :]


def _conv_mixer_kernel(xc_ref, xp_ref, g_ref, win_ref, cw_ref, cb_ref, wout_ref, gffn_ref, wr_ref,
                       o_ref, hn_ref, l_ref, u_ref, bg_ref, edge_ref):
    j = pl.program_id(1)
    nj = pl.num_programs(1) - 1
    ts, d = xc_ref.shape[1], xc_ref.shape[2]
    slot = j % 2

    def project():
        hn = _rms(xc_ref[0], g_ref[...]).astype(BF16)
        bg_ref[slot] = _dot(hn, win_ref[:, :d]).astype(BF16)
        u_new = _dot(hn, win_ref[:, d:2 * d]) * _dot(hn, win_ref[:, 2 * d:])
        u_ref[slot] = u_new
        return u_new[0:1, :]

    def finish(next_row):
        u = u_ref[1 - slot]
        prev_row = edge_ref[0:1, :]
        row = lax.broadcasted_iota(I32, u.shape, 0)
        u_prev = jnp.where(row == 0, prev_row, pltpu.roll(u, 1, axis=0))
        u_next = jnp.where(row == ts - 1, next_row, pltpu.roll(u, ts - 1, axis=0))
        cw = cw_ref[...]
        edge_ref[0:1, :] = u[ts - 1:ts, :]
        sub = ts // 2
        for r in range(0, ts, sub):
            rows = slice(r, r + sub)
            y = (cw[0:1, :] * u_prev[rows] + cw[1:2, :] * u[rows] + cw[2:3, :] * u_next[rows]
                 + cb_ref[...])
            z = (bg_ref[1 - slot, rows, :].astype(F32) * y).astype(BF16)
            h = xp_ref[0, rows, :] + _dot(z, wout_ref[...])
            o_ref[0, rows, :] = h
            _router_epilogue(h, gffn_ref, wr_ref, hn_ref, l_ref, rows)

    @pl.when(j == 0)
    def _():
        edge_ref[...] = jnp.zeros_like(edge_ref)
        project()

    @pl.when(jnp.logical_and(j > 0, j < nj))
    def _():
        finish(project())

    @pl.when(j == nj)
    def _():
        finish(jnp.zeros((1, d), F32))


def _conv_mixer(x, g, w_in, conv_w, conv_b, w_out, g_ffn, w_router, l_rows):
    b, s, d = x.shape
    ts = _row_tile(s, 512)
    nj = s // ts
    cur = lambda i, j: (i, jnp.minimum(j, nj - 1), 0)
    prev = lambda i, j: (i, jnp.maximum(j - 1, 0), 0)
    const = lambda i, j: (0, 0)
    return pl.pallas_call(
        _conv_mixer_kernel,
        grid=(b, nj + 1),
        in_specs=[
            pl.BlockSpec((1, ts, d), cur),
            pl.BlockSpec((1, ts, d), prev),
            pl.BlockSpec((1, d), const),
            pl.BlockSpec((d, 3 * d), const),
            pl.BlockSpec((3, d), const),
            pl.BlockSpec((1, d), const),
            pl.BlockSpec((d, d), const),
            pl.BlockSpec((1, d), const),
            pl.BlockSpec(w_router.shape, const),
        ],
        out_specs=[
            pl.BlockSpec((1, ts, d), prev),
            pl.BlockSpec((1, ts, d // 2), prev),
            pl.BlockSpec((1, l_rows, ts), lambda i, j: (i, 0, jnp.maximum(j - 1, 0))),
        ],
        out_shape=[
            jax.ShapeDtypeStruct((b, s, d), F32),
            jax.ShapeDtypeStruct((b, s, d // 2), I32),
            jax.ShapeDtypeStruct((b, l_rows, s), F32),
        ],
        scratch_shapes=[
            pltpu.VMEM((2, ts, d), F32),
            pltpu.VMEM((2, ts, d), BF16),
            pltpu.VMEM((SUBLANES, d), F32),
        ],
        compiler_params=_cparams("parallel", "arbitrary"),
        name="conv_mixer",
    )(x, x, g, w_in, conv_w, conv_b, w_out, g_ffn, w_router)


def _ret_in_kernel(x_ref, g_ref, wq_ref, wkt_ref, wv_ref, wg_ref,
                   cos_ref, sin_ref, cost_ref, sint_ref,
                   q_ref, kt_ref, v_ref, sg_ref):
    heads, tm, dk = q_ref.shape[1:]
    dv = v_ref.shape[3]
    half = dk // 2
    scale = dk ** -0.5
    sub = tm // 2
    for r in range(0, tm, sub):
        rows = slice(r, r + sub)
        hn = _rms(x_ref[0, rows, :], g_ref[...]).astype(BF16)
        q = _dot(hn, wq_ref[...])
        cos = cos_ref[rows, :]
        sin = sin_ref[rows, :]
        for h in range(heads):
            x1 = q[:, h * dk:h * dk + half]
            x2 = q[:, h * dk + half:(h + 1) * dk]
            q_ref[0, h, rows, :half] = (x1 * cos - x2 * sin).astype(BF16)
            q_ref[0, h, rows, half:] = (x1 * sin + x2 * cos).astype(BF16)
        kt = lax.dot_general(wkt_ref[...], hn, (((1,), (1,)), ((), ())),
                             preferred_element_type=F32)
        cost = cost_ref[:, rows]
        sint = sint_ref[:, rows]
        for h in range(heads):
            x1 = kt[h * dk:h * dk + half, :]
            x2 = kt[h * dk + half:(h + 1) * dk, :]
            kt_ref[0, h, :half, rows] = ((x1 * cost - x2 * sint) * scale).astype(BF16)
            kt_ref[0, h, half:, rows] = ((x1 * sint + x2 * cost) * scale).astype(BF16)
        v = _dot(hn, wv_ref[...])
        gate = _dot(hn, wg_ref[...])
        sg = gate * _sigmoid(gate)
        for h in range(heads):
            v_ref[0, h, rows, :] = v[:, h * dv:(h + 1) * dv].astype(BF16)
            sg_ref[0, h, rows, :] = sg[:, h * dv:(h + 1) * dv].astype(BF16)


def _ret_in(x, g, wq, wkt, wv, wg, cos, sin, heads):
    b, s, d = x.shape
    hq = wq.shape[1]
    hv = wv.shape[1]
    dk = hq // heads
    dv = hv // heads
    half = cos.shape[1]
    tm = _row_tile(s, 512)
    cost = cos.T
    sint = sin.T
    const = lambda i, j: (0, 0)
    return pl.pallas_call(
        _ret_in_kernel,
        grid=(b, s // tm),
        in_specs=[
            pl.BlockSpec((1, tm, d), lambda i, j: (i, j, 0)),
            pl.BlockSpec((1, d), const),
            pl.BlockSpec((d, hq), const),
            pl.BlockSpec((hq, d), const),
            pl.BlockSpec((d, hv), const),
            pl.BlockSpec((d, hv), const),
            pl.BlockSpec((tm, half), lambda i, j: (j, 0)),
            pl.BlockSpec((tm, half), lambda i, j: (j, 0)),
            pl.BlockSpec((half, tm), lambda i, j: (0, j)),
            pl.BlockSpec((half, tm), lambda i, j: (0, j)),
        ],
        out_specs=[
            pl.BlockSpec((1, heads, tm, dk), lambda i, j: (i, 0, j, 0)),
            pl.BlockSpec((1, heads, dk, tm), lambda i, j: (i, 0, 0, j)),
            pl.BlockSpec((1, heads, tm, dv), lambda i, j: (i, 0, j, 0)),
            pl.BlockSpec((1, heads, tm, dv), lambda i, j: (i, 0, j, 0)),
        ],
        out_shape=[
            jax.ShapeDtypeStruct((b, heads, s, dk), BF16),
            jax.ShapeDtypeStruct((b, heads, dk, s), BF16),
            jax.ShapeDtypeStruct((b, heads, s, dv), BF16),
            jax.ShapeDtypeStruct((b, heads, s, dv), BF16),
        ],
        compiler_params=_cparams("parallel", "parallel"),
        name="ret_in",
    )(x, g, wq, wkt, wv, wg, cos, sin, cost, sint)


def _retention_kernel(chunk, ld_ref, q_ref, kt_ref, v_ref, sg_ref, o_ref, acc_ref, sf_ref, sb_ref):
    h = pl.program_id(1)
    s = q_ref.shape[2]
    L = chunk
    nc = s // L
    lgf = ld_ref[0, h]
    lgb = ld_ref[1, h]
    ii = lax.broadcasted_iota(I32, (L, L), 0)
    jj = lax.broadcasted_iota(I32, (L, L), 1)
    dist = (ii - jj).astype(F32)
    causal = ii >= jj
    decay = jnp.where(causal,
                      jnp.exp(lgf * jnp.where(causal, dist, 0.0)),
                      jnp.exp(lgb * jnp.where(causal, 0.0, -dist)))
    col = lax.broadcasted_iota(I32, (L, 1), 0).astype(F32)
    row = lax.broadcasted_iota(I32, (1, L), 1).astype(F32)
    q_dec_f = jnp.exp(lgf * (col + 1.0)).astype(BF16)
    q_dec_b = jnp.exp(lgb * (L - col)).astype(BF16)
    k_dec_f = jnp.exp(lgf * (L - 1.0 - row)).astype(BF16)
    k_dec_b = jnp.exp(lgb * row).astype(BF16)
    one = jnp.ones((1, 1), F32)
    c_dec_f = jnp.exp(lgf * L * one)
    c_dec_b = jnp.exp(lgb * L * one)

    def finish(c, out):
        sl = slice(c * L, (c + 1) * L)
        mu = jnp.mean(out, axis=-1, keepdims=True)
        cen = out - mu
        var = jnp.mean(cen * cen, axis=-1, keepdims=True)
        o_ref[0, 0, sl, :] = (sg_ref[0, 0, sl, :].astype(F32) * (cen * lax.rsqrt(var + GN_EPS))
                              ).astype(BF16)

    stored = {}

    def contribute(c, part):
        sl = slice(c * L, (c + 1) * L)
        if c not in stored:
            stored[c] = part is not None
            if part is not None:
                acc_ref[sl, :] = part
        elif not stored[c]:
            finish(c, part)
        else:
            finish(c, acc_ref[sl, :] if part is None else acc_ref[sl, :] + part)

    for k in range(nc):
        cf, cb = k, nc - 1 - k
        slf = slice(cf * L, (cf + 1) * L)
        slb = slice(cb * L, (cb + 1) * L)
        qc, ktc, vc = q_ref[0, 0, slf, :], kt_ref[0, 0, :, slf], v_ref[0, 0, slf, :]
        scores = (_dot(qc, ktc) * decay).astype(BF16)
        part_f = _dot(scores, vc)
        if cf > 0:
            part_f += _dot(qc * q_dec_f, sf_ref[...].astype(BF16))
        if cf < nc - 1:
            upd = _dot(ktc * k_dec_f, vc)
            sf_ref[...] = upd if cf == 0 else sf_ref[...] * c_dec_f + upd
        part_b = None
        if cb < nc - 1:
            part_b = _dot(q_ref[0, 0, slb, :] * q_dec_b, sb_ref[...].astype(BF16))
        if cb > 0:
            upd = _dot(kt_ref[0, 0, :, slb] * k_dec_b, v_ref[0, 0, slb, :])
            sb_ref[...] = upd if cb == nc - 1 else sb_ref[...] * c_dec_b + upd
        contribute(cf, part_f)
        contribute(cb, part_b)


def _retention(q, kt, v, sg, log_decay):
    b, heads, s, dk = q.shape
    dv = v.shape[3]
    chunk = min(RET_CHUNK, s)
    return pl.pallas_call(
        functools.partial(_retention_kernel, chunk),
        grid=(b, heads),
        in_specs=[
            pl.BlockSpec(memory_space=pltpu.SMEM),
            pl.BlockSpec((1, 1, s, dk), lambda i, h: (i, h, 0, 0)),
            pl.BlockSpec((1, 1, dk, s), lambda i, h: (i, h, 0, 0)),
            pl.BlockSpec((1, 1, s, dv), lambda i, h: (i, h, 0, 0)),
            pl.BlockSpec((1, 1, s, dv), lambda i, h: (i, h, 0, 0)),
        ],
        out_specs=pl.BlockSpec((1, 1, s, dv), lambda i, h: (i, h, 0, 0)),
        out_shape=jax.ShapeDtypeStruct((b, heads, s, dv), BF16),
        scratch_shapes=[
            pltpu.VMEM((s, dv), F32),
            pltpu.VMEM((dk, dv), F32),
            pltpu.VMEM((dk, dv), F32),
        ],
        compiler_params=_cparams("parallel", "parallel"),
        name="retention",
    )(log_decay, q, kt, v, sg)


def _ret_out_kernel(o_ref, w_ref, x_ref, gffn_ref, wr_ref, h_ref, hn_ref, l_ref):
    heads, ts = o_ref.shape[1], o_ref.shape[2]
    sub = ts // 2
    for r in range(0, ts, sub):
        rows = slice(r, r + sub)
        h = x_ref[0, rows, :]
        for hd in range(heads):
            h = h + _dot(o_ref[0, hd, rows, :], w_ref[hd])
        h_ref[0, rows, :] = h
        _router_epilogue(h, gffn_ref, wr_ref, hn_ref, l_ref, rows)


def _ret_out(o, w_out, x, g_ffn, w_router, l_rows):
    b, heads, s, dv = o.shape
    d = x.shape[2]
    ts = _row_tile(s, 512)
    tile = lambda i, j: (i, j, 0)
    const = lambda i, j: (0, 0)
    return pl.pallas_call(
        _ret_out_kernel,
        grid=(b, s // ts),
        in_specs=[
            pl.BlockSpec((1, heads, ts, dv), lambda i, j: (i, 0, j, 0)),
            pl.BlockSpec((heads, dv, d), lambda i, j: (0, 0, 0)),
            pl.BlockSpec((1, ts, d), tile),
            pl.BlockSpec((1, d), const),
            pl.BlockSpec(w_router.shape, const),
        ],
        out_specs=[
            pl.BlockSpec((1, ts, d), tile),
            pl.BlockSpec((1, ts, d // 2), tile),
            pl.BlockSpec((1, l_rows, ts), lambda i, j: (i, 0, j)),
        ],
        out_shape=[
            jax.ShapeDtypeStruct((b, s, d), F32),
            jax.ShapeDtypeStruct((b, s, d // 2), I32),
            jax.ShapeDtypeStruct((b, l_rows, s), F32),
        ],
        compiler_params=_cparams("parallel", "parallel"),
        name="ret_out",
    )(o, w_out.reshape(heads, dv, d), x, g_ffn, w_router)


def _exclusive_prefix(flags):
    rows, s = flags.shape
    upper = (lax.broadcasted_iota(I32, (LANES, LANES), 0)
             < lax.broadcasted_iota(I32, (LANES, LANES), 1))
    upper = jnp.where(upper, 1.0, 0.0).astype(BF16)
    carry = jnp.zeros((rows, 1), F32)
    pieces = []
    for k in range(s // LANES):
        blk = flags[:, k * LANES:(k + 1) * LANES]
        pieces.append(_dot(blk.astype(BF16), upper) + carry)
        carry = carry + jnp.sum(blk, axis=1, keepdims=True)
    return jnp.concatenate(pieces, axis=1)


def _router_select_kernel(n_exp, cap, l_ref, rank_ref, gate_ref):
    nb = l_ref.shape[0]
    affs = []
    for i in range(nb):
        l = l_ref[i]
        logits = l[0:n_exp] + l[n_exp:2 * n_exp] + l[2 * n_exp:3 * n_exp]
        e = jnp.exp(logits - jnp.max(logits, axis=0, keepdims=True))
        affs.append(e / jnp.sum(e, axis=0, keepdims=True))
    aff = jnp.concatenate(affs, axis=0)
    keys = pltpu.bitcast(aff, I32)

    def refine(i, thr):
        cand = thr | jnp.left_shift(jnp.int32(1), 30 - i)
        cnt = jnp.sum(jnp.where(keys >= cand, 1.0, 0.0), axis=1, keepdims=True)
        return jnp.where(cnt >= cap, cand, thr)

    thr = lax.fori_loop(0, 31, refine, jnp.zeros((nb * n_exp, 1), I32))
    above = keys > thr
    tied = keys == thr
    n_above = jnp.sum(jnp.where(above, 1.0, 0.0), axis=1, keepdims=True)
    tied_rank = _exclusive_prefix(jnp.where(tied, 1.0, 0.0))
    keep = above | (tied & (tied_rank < cap - n_above))
    rank = _exclusive_prefix(jnp.where(keep, 1.0, 0.0))
    rank = jnp.where(keep, rank.astype(I32), -1)
    gate = jnp.where(keep, aff, 0.0)
    for i in range(nb):
        rank_ref[i] = rank[i * n_exp:(i + 1) * n_exp]
        gate_ref[i] = gate[i * n_exp:(i + 1) * n_exp]


def _router_select(lt, n_exp, cap):
    b, rows, s = lt.shape
    nb = _row_tile(b, 4)
    blk = lambda i: (i, 0, 0)
    return pl.pallas_call(
        functools.partial(_router_select_kernel, n_exp, cap),
        grid=(b // nb,),
        in_specs=[pl.BlockSpec((nb, rows, s), blk)],
        out_specs=[
            pl.BlockSpec((nb, n_exp, s), blk),
            pl.BlockSpec((nb, n_exp, s), blk),
        ],
        out_shape=[
            jax.ShapeDtypeStruct((b, n_exp, s), I32),
            jax.ShapeDtypeStruct((b, n_exp, s), F32),
        ],
        compiler_params=_cparams("parallel"),
        name="router_select",
    )(lt)


def _gather_sc_kernel(n_seq, seq_len, n_exp, e0, ne, cap, win, n_cores, pairs_per_worker,
                      table_hbm, rank_hbm, gate_hbm, zeros_hbm, xe_hbm, gs_hbm,
                      rank_v, gate_v, tok_v, rows_v, gs_v, gsem, osem):
    lanes = SC_LANES
    wid = lax.axis_index("s") * n_cores + lax.axis_index("c")
    lane = lax.broadcasted_iota(I32, (lanes,), 0)
    lane0 = jnp.zeros((lanes,), I32)
    pltpu.sync_copy(zeros_hbm, gs_v)
    n_win = cap // win

    @pl.loop(0, pairs_per_worker)
    def _(i):
        pair = wid * pairs_per_worker + i
        b = pair // ne
        e = pair % ne
        row = b * n_exp + e0 + e
        pltpu.sync_copy(rank_hbm.at[row], rank_v)
        pltpu.sync_copy(gate_hbm.at[row], gate_v)

        @pl.loop(0, seq_len // lanes)
        def _(j):
            r = rank_v[pl.ds(j * lanes, lanes)]
            keep = r >= 0
            slot = jnp.where(keep, r, 0)
            plsc.store_scatter(tok_v, [slot // win, slot % win], lane + (j * lanes + b * seq_len),
                               mask=keep)
            plsc.store_scatter(gs_v, [slot, lane0], gate_v[pl.ds(j * lanes, lanes)], mask=keep)

        out_base = (e * n_seq + b) * cap

        def fetch(w):
            return pltpu.make_async_copy(table_hbm.at[tok_v.at[w]], rows_v.at[w % 2], gsem.at[w % 2])

        def put(w):
            return pltpu.make_async_copy(rows_v.at[w % 2], xe_hbm.at[pl.ds(out_base + w * win, win)],
                                         osem.at[w % 2])

        fetch(0).start()
        for w in range(n_win):
            if w + 1 < n_win:
                if w >= 1:
                    put(w - 1).wait()
                fetch(w + 1).start()
            fetch(w).wait()
            put(w).start()
        if n_win >= 2:
            put(n_win - 2).wait()
        put(n_win - 1).wait()
        pltpu.sync_copy(gs_v, gs_hbm.at[pl.ds(out_base, cap)])


def _gather_sc(table, rank, gate, n_seq, seq_len, n_exp, e0, ne, cap):
    width = table.shape[1]
    info = plsc.get_sparse_core_info()
    n_workers = info.num_cores * info.num_subcores
    pairs = n_seq * ne
    assert info.num_lanes == SC_LANES and pairs % n_workers == 0 and seq_len % SC_LANES == 0
    win = min(cap, SC_GATHER_WINDOW)
    assert cap % win == 0
    mesh = plsc.VectorSubcoreMesh(core_axis_name="c", subcore_axis_name="s")
    call = pl.kernel(
        functools.partial(_gather_sc_kernel, n_seq, seq_len, n_exp, e0, ne, cap, win,
                          info.num_cores, pairs // n_workers),
        mesh=mesh,
        out_type=[jax.ShapeDtypeStruct((ne * n_seq * cap, width), I32),
                  jax.ShapeDtypeStruct((ne * n_seq * cap, LANES), F32)],
        scratch_types=[
            pltpu.VMEM((seq_len,), I32),
            pltpu.VMEM((seq_len,), F32),
            pltpu.VMEM((cap // win, win), I32),
            pltpu.VMEM((2, win, width), I32),
            pltpu.VMEM((cap, LANES), F32),
            pltpu.SemaphoreType.DMA((2,)),
            pltpu.SemaphoreType.DMA((2,)),
        ],
        compiler_params=pltpu.CompilerParams(needs_layout_passes=False),
        name="moe_gather_sc",
    )
    return call(table, rank, gate, jnp.zeros((cap, LANES), F32))


def _ffn_kernel(layer, e0, xe_ref, gs_ref, wg_hbm, wu_hbm, wd_hbm, y_ref,
                wg_s, wu_s, wd_s, stg_g, stg_u, stg_d, sem):
    e = pl.program_id(0)
    m = pl.program_id(1)
    n_exp = pl.num_programs(0)
    nm = pl.num_programs(1)
    n_chunks, d, fc = wg_s.shape[1:]
    cur = e % 2
    par = m % 2

    def chunk_copies(exp, k, buf):
        cols = pl.ds(pl.multiple_of(k * fc, fc), fc)
        return (pltpu.make_async_copy(wg_hbm.at[layer, e0 + exp, :, cols], stg_g.at[buf], sem.at[buf, 0]),
                pltpu.make_async_copy(wu_hbm.at[layer, e0 + exp, :, cols], stg_u.at[buf], sem.at[buf, 1]),
                pltpu.make_async_copy(wd_hbm.at[layer, e0 + exp, cols, :], stg_d.at[buf], sem.at[buf, 2]))

    def convert(buf, slot, k):
        wg_s[slot, k] = stg_g[buf].astype(BF16)
        wu_s[slot, k] = stg_u[buf].astype(BF16)
        wd_s[slot, pl.ds(pl.multiple_of(k * fc, fc), fc), :] = stg_d[buf].astype(BF16)

    first = jnp.logical_and(e == 0, m == 0)

    @pl.when(first)
    def _():
        for c in chunk_copies(0, 0, 0):
            c.start()
        for k in range(n_chunks):
            if k + 1 < n_chunks:
                for c in chunk_copies(0, k + 1, (k + 1) % 2):
                    c.start()
            for c in chunk_copies(0, k, k % 2):
                c.wait()
            if k < n_chunks - 1:
                convert(k % 2, 0, k)

    in_flight = jnp.where(m == 0, e > 0, e < n_exp - 1)

    @pl.when(in_flight)
    def _():
        for c in chunk_copies(0, 0, 1 - par):
            c.wait()

    @pl.when(e < n_exp - 1)
    def _():
        for c in chunk_copies(e + 1, m, par):
            c.start()

    convert(1 - par, jnp.where(m == 0, cur, 1 - cur), jnp.where(m == 0, nm - 1, m - 1))

    x = _unpack_bf16_pairs(xe_ref[0])
    acc = jnp.zeros((x.shape[0], d), F32)
    for k in range(0, n_chunks, 2):
        acts = []
        for kk in (k, k + 1):
            a = _dot(x, wg_s[cur, kk])
            acts.append(a * _sigmoid(a) * _dot(x, wu_s[cur, kk]))
        act = jnp.concatenate(acts, axis=1).astype(BF16)
        acc += _dot(act, wd_s[cur, k * fc:(k + 2) * fc, :])
    y_ref[0] = (acc * gs_ref[0][:, 0:1]).astype(BF16)


def _ffn(layer, e0, xe, gs, wg, wu, wd):
    n_exp, rows, half_d = xe.shape
    d = 2 * half_d
    ff = wg.shape[3]
    fc = min(ff // 2, FFN_WEIGHT_CHUNK)
    nm = ff // fc
    assert ff % fc == 0 and nm % 2 == 0 and rows % nm == 0
    tm = rows // nm
    assert tm % (2 * SUBLANES) == 0
    return pl.pallas_call(
        functools.partial(_ffn_kernel, layer, e0),
        grid=(n_exp, nm),
        in_specs=[
            pl.BlockSpec((1, tm, half_d), lambda e, m: (e, m, 0)),
            pl.BlockSpec((1, tm, LANES), lambda e, m: (e, m, 0)),
            pl.BlockSpec(memory_space=pl.ANY),
            pl.BlockSpec(memory_space=pl.ANY),
            pl.BlockSpec(memory_space=pl.ANY),
        ],
        out_specs=pl.BlockSpec((1, tm, d), lambda e, m: (e, m, 0)),
        out_shape=jax.ShapeDtypeStruct((n_exp, rows, d), BF16),
        scratch_shapes=[
            pltpu.VMEM((2, nm, d, fc), BF16),
            pltpu.VMEM((2, nm, d, fc), BF16),
            pltpu.VMEM((2, ff, d), BF16),
            pltpu.VMEM((2, d, fc), F32),
            pltpu.VMEM((2, d, fc), F32),
            pltpu.VMEM((2, fc, d), F32),
            pltpu.SemaphoreType.DMA((2, 3)),
        ],
        compiler_params=_cparams("arbitrary", "arbitrary"),
        name="moe_ffn",
    )(xe, gs, wg, wu, wd)


def _combine_ple_kernel(final, n_groups, h_ref, *refs):
    y_refs = refs[:n_groups]
    rankt_ref, p_ref, gple_ref, wgate_ref, wproj_ref, gfin_ref, o_ref = refs[n_groups:]
    _, _, cap, d = y_refs[0].shape
    ts = h_ref.shape[1]
    rt = rankt_ref[0]
    slot = lax.broadcasted_iota(I32, (ts, cap), 1)
    h = h_ref[0]
    e0 = 0
    for y_ref in y_refs:
        ne = y_ref.shape[0]
        onehot = jnp.concatenate(
            [jnp.where(rt[:, e:e + 1] == slot, 1.0, 0.0).astype(BF16) for e in range(e0, e0 + ne)],
            axis=1)
        h = h + _dot(onehot, y_ref[:, 0].reshape(ne * cap, d))
        e0 += ne
    gate = _sigmoid(_dot(_rms(h, gple_ref[...]).astype(BF16), wgate_ref[...]))
    h = h + gate * _dot(p_ref[...].astype(BF16), wproj_ref[...])
    if final:
        h = _rms(h, gfin_ref[...])
    o_ref[0] = h


def _combine_ple(layer, b0, h, ys, rankt, p, g_ple, w_gate, w_proj, g_final, final):
    b, s, d = h.shape
    cap = ys[0].shape[2]
    n_exp = sum(y.shape[0] for y in ys)
    ple = p.shape[3]
    ts = _row_tile(s, 512)
    const = lambda i, j: (0, 0)
    return pl.pallas_call(
        functools.partial(_combine_ple_kernel, final, len(ys)),
        grid=(b, s // ts),
        in_specs=[
            pl.BlockSpec((1, ts, d), lambda i, j: (i, j, 0)),
            *[pl.BlockSpec((y.shape[0], 1, cap, d), lambda i, j: (0, i, 0, 0)) for y in ys],
            pl.BlockSpec((1, ts, n_exp), lambda i, j: (i, j, 0)),
            pl.BlockSpec((None, None, ts, ple), lambda i, j: (layer, b0 + i, j, 0)),
            pl.BlockSpec((1, d), const),
            pl.BlockSpec((d, d), const),
            pl.BlockSpec((ple, d), const),
            pl.BlockSpec((1, d), const),
        ],
        out_specs=pl.BlockSpec((1, ts, d), lambda i, j: (i, j, 0)),
        out_shape=jax.ShapeDtypeStruct((b, s, d), F32),
        compiler_params=_cparams("parallel", "arbitrary"),
        name="combine_ple",
    )(h, *ys, rankt, p, g_ple, w_gate, w_proj, g_final)


def _split_router_weight(w):
    d, n_exp = w.shape
    hi = w.astype(BF16)
    r1 = w - hi.astype(F32)
    mid = r1.astype(BF16)
    lo = (r1 - mid.astype(F32)).astype(BF16)
    pad = jnp.zeros((d, LANES - 3 * n_exp), BF16)
    return jnp.concatenate([hi, mid, lo, pad], axis=1)


def _moe_ple(layer, b0, h, hn, lt, p, n_exp, w_gate, w_up, w_down, g_ple, ple_w_gate, ple_w_proj,
             g_final, final):
    b, s, d = h.shape
    cap = EC_CAPACITY_FACTOR * s // n_exp
    rank, gate = _router_select(lt, n_exp, cap)
    split = n_exp - EXPERT_GROUP_SPLIT if 0 < EXPERT_GROUP_SPLIT < n_exp else n_exp
    groups = [(0, split)] + ([(split, n_exp - split)] if split < n_exp else [])
    table = hn.reshape(b * s, d // 2)
    rank2 = rank.reshape(b * n_exp, s)
    gate2 = gate.reshape(b * n_exp, s)
    gathered = [_gather_sc(table, rank2, gate2, b, s, n_exp, e0, ne, cap) for e0, ne in groups]
    ys = []
    for (e0, ne), (xe, gs) in zip(groups, gathered):
        y = _ffn(layer, e0, xe.reshape(ne, b * cap, d // 2), gs.reshape(ne, b * cap, LANES),
                 w_gate, w_up, w_down)
        ys.append(y.reshape(ne, b, cap, d))
    rankt = jnp.swapaxes(rank, 1, 2)
    return _combine_ple(layer, b0, h, ys, rankt, p, g_ple, ple_w_gate, ple_w_proj, g_final, final)


def _rope_tables(s, half):
    inv = ROPE_BASE ** (-jnp.arange(half, dtype=F32) / half)
    ang = jnp.arange(s).astype(F32)[:, None] * inv[None, :]
    return jnp.cos(ang), jnp.sin(ang)


def _forward(x, b0, p, norm_mix, norm_ffn, norm_ple, final_norm, conv_w_in, conv_w, conv_b,
             conv_w_out, ret_w_in, ret_log_decay, ret_w_out, router_w, exp_w_gate, exp_w_up,
             exp_w_down, ple_w_proj, ple_w_gate):
    b, s, d = x.shape
    depth = p.shape[0]
    heads = ret_log_decay.shape[2]
    n_exp = router_w.shape[2]
    hq = d
    hv = (ret_w_in.shape[2] - 2 * hq) // 2
    h = x
    for i in range(depth):
        j = i // 2
        g_mix = norm_mix[i][None, :]
        g_ffn = norm_ffn[i][None, :]
        w_router = _split_router_weight(router_w[i])
        if i % 2 == 0:
            h, hn, lt = _conv_mixer(h, g_mix, conv_w_in[j].astype(BF16), conv_w[j],
                                      conv_b[j][None, :], conv_w_out[j].astype(BF16),
                                      g_ffn, w_router, 3 * n_exp)
        else:
            w_in = ret_w_in[j]
            wq = w_in[:, :hq].astype(BF16)
            wkt = w_in[:, hq:2 * hq].T.astype(BF16)
            wv = w_in[:, 2 * hq:2 * hq + hv].astype(BF16)
            wg = w_in[:, 2 * hq + hv:].astype(BF16)
            cos, sin = _rope_tables(s, hq // heads // 2)
            q, kt, v, sg = _ret_in(h, g_mix, wq, wkt, wv, wg, cos, sin, heads)
            o = _retention(q, kt, v, sg, ret_log_decay[j])
            h, hn, lt = _ret_out(o, ret_w_out[j].astype(BF16), h, g_ffn, w_router, 3 * n_exp)
        h = _moe_ple(i, b0, h, hn, lt, p, n_exp, exp_w_gate, exp_w_up, exp_w_down,
                     norm_ple[i][None, :],
                     ple_w_gate[i].astype(BF16), ple_w_proj[i].astype(BF16),
                     final_norm[None, :], i == depth - 1)
    return h


def kernel(x, p, norm_mix, norm_ffn, norm_ple, final_norm, conv_w_in, conv_w, conv_b, conv_w_out,
           ret_w_in, ret_log_decay, ret_w_out, router_w, exp_w_gate, exp_w_up, exp_w_down,
           ple_w_proj, ple_w_gate):
    return _forward(x, 0, p, norm_mix, norm_ffn, norm_ple, final_norm, conv_w_in, conv_w, conv_b,
                    conv_w_out, ret_w_in, ret_log_decay, ret_w_out, router_w, exp_w_gate, exp_w_up,
                    exp_w_down, ple_w_proj, ple_w_gate)
```

```python
import functools

import jax
import jax.numpy as jnp
from jax import lax
from jax.experimental import pallas as pl
from jax.experimental.pallas import tpu as pltpu
from jax.experimental.pallas import tpu_sc as plsc

F32 = jnp.float32
BF16 = jnp.bfloat16
I32 = jnp.int32

NORM_EPS = 1e-6
GN_EPS = 1e-5
ROPE_BASE = 10000.0
EC_CAPACITY_FACTOR = 2
RET_CHUNK = 256
LANES = 128
SUBLANES = 8
SC_LANES = 16
SC_GATHER_WINDOW = 64
EXPERT_GROUP_SPLIT = 4
RET_OUT_SUBTILE = 256
FFN_WEIGHT_CHUNK = 256
VMEM_LIMIT = 56 * 1024 * 1024


def _cparams(*sem):
    return pltpu.CompilerParams(dimension_semantics=sem, vmem_limit_bytes=VMEM_LIMIT)


def _row_tile(n, target):
    t = min(n, target)
    while n % t:
        t //= 2
    return t


def _rms(x, g):
    return x * lax.rsqrt(jnp.mean(x * x, axis=-1, keepdims=True) + NORM_EPS) * g


def _sigmoid(x):
    return 1.0 / (1.0 + jnp.exp(-x))


def _dot(a, b):
    return jnp.dot(a, b, preferred_element_type=F32)


def _pack_bf16_pairs(x):
    w = x.shape[1] // 2
    lo = lax.bitcast_convert_type(x[:, :w].astype(BF16).astype(F32), I32)
    hi = lax.bitcast_convert_type(x[:, w:].astype(BF16).astype(F32), I32)
    return lax.shift_right_logical(lo, jnp.int32(16)) | (hi & jnp.int32(-65536))


def _unpack_bf16_pairs(p):
    lo = lax.bitcast_convert_type(lax.shift_left(p, jnp.int32(16)), F32).astype(BF16)
    hi = lax.bitcast_convert_type(p & jnp.int32(-65536), F32).astype(BF16)
    return jnp.concatenate([lo, hi], axis=1)


def _router_epilogue(h, g_ref, wr_ref, hn_ref, l_ref, rows=slice(None)):
    hn = _rms(h, g_ref[...])
    hi = hn.astype(BF16)
    lo = (hn - hi.astype(F32)).astype(BF16)
    hn_ref[0, rows, :] = _pack_bf16_pairs(hn)
    logits = _dot(hi, wr_ref[...]) + _dot(lo, wr_ref[...])
    l_ref[0, :, rows] = logits.T[:l_ref.shape[1], :]


def _conv_mixer_kernel(xc_ref, xp_ref, g_ref, win_ref, cw_ref, cb_ref, wout_ref, gffn_ref, wr_ref,
                       o_ref, hn_ref, l_ref, u_ref, bg_ref, edge_ref):
    j = pl.program_id(1)
    nj = pl.num_programs(1) - 1
    ts, d = xc_ref.shape[1], xc_ref.shape[2]
    slot = j % 2

    def project():
        hn = _rms(xc_ref[0], g_ref[...]).astype(BF16)
        bg_ref[slot] = _dot(hn, win_ref[:, :d]).astype(BF16)
        u_new = _dot(hn, win_ref[:, d:2 * d]) * _dot(hn, win_ref[:, 2 * d:])
        u_ref[slot] = u_new
        return u_new[0:1, :]

    def finish(next_row):
        u = u_ref[1 - slot]
        prev_row = edge_ref[0:1, :]
        row = lax.broadcasted_iota(I32, u.shape, 0)
        u_prev = jnp.where(row == 0, prev_row, pltpu.roll(u, 1, axis=0))
        u_next = jnp.where(row == ts - 1, next_row, pltpu.roll(u, ts - 1, axis=0))
        cw = cw_ref[...]
        edge_ref[0:1, :] = u[ts - 1:ts, :]
        sub = ts // 2
        for r in range(0, ts, sub):
            rows = slice(r, r + sub)
            y = (cw[0:1, :] * u_prev[rows] + cw[1:2, :] * u[rows] + cw[2:3, :] * u_next[rows]
                 + cb_ref[...])
            z = (bg_ref[1 - slot, rows, :].astype(F32) * y).astype(BF16)
            h = xp_ref[0, rows, :] + _dot(z, wout_ref[...])
            o_ref[0, rows, :] = h
            _router_epilogue(h, gffn_ref, wr_ref, hn_ref, l_ref, rows)

    @pl.when(j == 0)
    def _():
        edge_ref[...] = jnp.zeros_like(edge_ref)
        project()

    @pl.when(jnp.logical_and(j > 0, j < nj))
    def _():
        finish(project())

    @pl.when(j == nj)
    def _():
        finish(jnp.zeros((1, d), F32))


def _conv_mixer(x, g, w_in, conv_w, conv_b, w_out, g_ffn, w_router, l_rows):
    b, s, d = x.shape
    ts = _row_tile(s, 512)
    nj = s // ts
    cur = lambda i, j: (i, jnp.minimum(j, nj - 1), 0)
    prev = lambda i, j: (i, jnp.maximum(j - 1, 0), 0)
    const = lambda i, j: (0, 0)
    return pl.pallas_call(
        _conv_mixer_kernel,
        grid=(b, nj + 1),
        in_specs=[
            pl.BlockSpec((1, ts, d), cur),
            pl.BlockSpec((1, ts, d), prev),
            pl.BlockSpec((1, d), const),
            pl.BlockSpec((d, 3 * d), const),
            pl.BlockSpec((3, d), const),
            pl.BlockSpec((1, d), const),
            pl.BlockSpec((d, d), const),
            pl.BlockSpec((1, d), const),
            pl.BlockSpec(w_router.shape, const),
        ],
        out_specs=[
            pl.BlockSpec((1, ts, d), prev),
            pl.BlockSpec((1, ts, d // 2), prev),
            pl.BlockSpec((1, l_rows, ts), lambda i, j: (i, 0, jnp.maximum(j - 1, 0))),
        ],
        out_shape=[
            jax.ShapeDtypeStruct((b, s, d), F32),
            jax.ShapeDtypeStruct((b, s, d // 2), I32),
            jax.ShapeDtypeStruct((b, l_rows, s), F32),
        ],
        scratch_shapes=[
            pltpu.VMEM((2, ts, d), F32),
            pltpu.VMEM((2, ts, d), BF16),
            pltpu.VMEM((SUBLANES, d), F32),
        ],
        compiler_params=_cparams("parallel", "arbitrary"),
        name="conv_mixer",
    )(x, x, g, w_in, conv_w, conv_b, w_out, g_ffn, w_router)


def _ret_in_kernel(x_ref, g_ref, wq_ref, wkt_ref, wv_ref, wg_ref,
                   cos_ref, sin_ref, cost_ref, sint_ref,
                   q_ref, kt_ref, v_ref, sg_ref):
    heads, tm, dk = q_ref.shape[1:]
    dv = v_ref.shape[3]
    half = dk // 2
    scale = dk ** -0.5
    sub = tm // 2
    for r in range(0, tm, sub):
        rows = slice(r, r + sub)
        hn = _rms(x_ref[0, rows, :], g_ref[...]).astype(BF16)
        q = _dot(hn, wq_ref[...])
        cos = cos_ref[rows, :]
        sin = sin_ref[rows, :]
        for h in range(heads):
            x1 = q[:, h * dk:h * dk + half]
            x2 = q[:, h * dk + half:(h + 1) * dk]
            q_ref[0, h, rows, :half] = (x1 * cos - x2 * sin).astype(BF16)
            q_ref[0, h, rows, half:] = (x1 * sin + x2 * cos).astype(BF16)
        kt = lax.dot_general(wkt_ref[...], hn, (((1,), (1,)), ((), ())),
                             preferred_element_type=F32)
        cost = cost_ref[:, rows]
        sint = sint_ref[:, rows]
        for h in range(heads):
            x1 = kt[h * dk:h * dk + half, :]
            x2 = kt[h * dk + half:(h + 1) * dk, :]
            kt_ref[0, h, :half, rows] = ((x1 * cost - x2 * sint) * scale).astype(BF16)
            kt_ref[0, h, half:, rows] = ((x1 * sint + x2 * cost) * scale).astype(BF16)
        v = _dot(hn, wv_ref[...])
        gate = _dot(hn, wg_ref[...])
        sg = gate * _sigmoid(gate)
        for h in range(heads):
            v_ref[0, h, rows, :] = v[:, h * dv:(h + 1) * dv].astype(BF16)
            sg_ref[0, h, rows, :] = sg[:, h * dv:(h + 1) * dv].astype(BF16)


def _ret_in(x, g, wq, wkt, wv, wg, cos, sin, heads):
    b, s, d = x.shape
    hq = wq.shape[1]
    hv = wv.shape[1]
    dk = hq // heads
    dv = hv // heads
    half = cos.shape[1]
    tm = _row_tile(s, 512)
    cost = cos.T
    sint = sin.T
    const = lambda i, j: (0, 0)
    return pl.pallas_call(
        _ret_in_kernel,
        grid=(b, s // tm),
        in_specs=[
            pl.BlockSpec((1, tm, d), lambda i, j: (i, j, 0)),
            pl.BlockSpec((1, d), const),
            pl.BlockSpec((d, hq), const),
            pl.BlockSpec((hq, d), const),
            pl.BlockSpec((d, hv), const),
            pl.BlockSpec((d, hv), const),
            pl.BlockSpec((tm, half), lambda i, j: (j, 0)),
            pl.BlockSpec((tm, half), lambda i, j: (j, 0)),
            pl.BlockSpec((half, tm), lambda i, j: (0, j)),
            pl.BlockSpec((half, tm), lambda i, j: (0, j)),
        ],
        out_specs=[
            pl.BlockSpec((1, heads, tm, dk), lambda i, j: (i, 0, j, 0)),
            pl.BlockSpec((1, heads, dk, tm), lambda i, j: (i, 0, 0, j)),
            pl.BlockSpec((1, heads, tm, dv), lambda i, j: (i, 0, j, 0)),
            pl.BlockSpec((1, heads, tm, dv), lambda i, j: (i, 0, j, 0)),
        ],
        out_shape=[
            jax.ShapeDtypeStruct((b, heads, s, dk), BF16),
            jax.ShapeDtypeStruct((b, heads, dk, s), BF16),
            jax.ShapeDtypeStruct((b, heads, s, dv), BF16),
            jax.ShapeDtypeStruct((b, heads, s, dv), BF16),
        ],
        compiler_params=_cparams("parallel", "parallel"),
        name="ret_in",
    )(x, g, wq, wkt, wv, wg, cos, sin, cost, sint)


def _retention_kernel(chunk, ld_ref, q_ref, kt_ref, v_ref, sg_ref, o_ref, acc_ref, sf_ref, sb_ref):
    h = pl.program_id(1)
    s = q_ref.shape[2]
    L = chunk
    nc = s // L
    lgf = ld_ref[0, h]
    lgb = ld_ref[1, h]
    ii = lax.broadcasted_iota(I32, (L, L), 0)
    jj = lax.broadcasted_iota(I32, (L, L), 1)
    dist = (ii - jj).astype(F32)
    causal = ii >= jj
    decay = jnp.where(causal,
                      jnp.exp(lgf * jnp.where(causal, dist, 0.0)),
                      jnp.exp(lgb * jnp.where(causal, 0.0, -dist)))
    col = lax.broadcasted_iota(I32, (L, 1), 0).astype(F32)
    row = lax.broadcasted_iota(I32, (1, L), 1).astype(F32)
    q_dec_f = jnp.exp(lgf * (col + 1.0)).astype(BF16)
    q_dec_b = jnp.exp(lgb * (L - col)).astype(BF16)
    k_dec_f = jnp.exp(lgf * (L - 1.0 - row)).astype(BF16)
    k_dec_b = jnp.exp(lgb * row).astype(BF16)
    one = jnp.ones((1, 1), F32)
    c_dec_f = jnp.exp(lgf * L * one)
    c_dec_b = jnp.exp(lgb * L * one)

    def finish(c, out):
        sl = slice(c * L, (c + 1) * L)
        mu = jnp.mean(out, axis=-1, keepdims=True)
        cen = out - mu
        var = jnp.mean(cen * cen, axis=-1, keepdims=True)
        o_ref[0, 0, sl, :] = (sg_ref[0, 0, sl, :].astype(F32) * (cen * lax.rsqrt(var + GN_EPS))
                              ).astype(BF16)

    stored = {}

    def contribute(c, part):
        sl = slice(c * L, (c + 1) * L)
        if c not in stored:
            stored[c] = part is not None
            if part is not None:
                acc_ref[sl, :] = part
        elif not stored[c]:
            finish(c, part)
        else:
            finish(c, acc_ref[sl, :] if part is None else acc_ref[sl, :] + part)

    for k in range(nc):
        cf, cb = k, nc - 1 - k
        slf = slice(cf * L, (cf + 1) * L)
        slb = slice(cb * L, (cb + 1) * L)
        qc, ktc, vc = q_ref[0, 0, slf, :], kt_ref[0, 0, :, slf], v_ref[0, 0, slf, :]
        scores = (_dot(qc, ktc) * decay).astype(BF16)
        part_f = _dot(scores, vc)
        if cf > 0:
            part_f += _dot(qc * q_dec_f, sf_ref[...].astype(BF16))
        if cf < nc - 1:
            upd = _dot(ktc * k_dec_f, vc)
            sf_ref[...] = upd if cf == 0 else sf_ref[...] * c_dec_f + upd
        part_b = None
        if cb < nc - 1:
            part_b = _dot(q_ref[0, 0, slb, :] * q_dec_b, sb_ref[...].astype(BF16))
        if cb > 0:
            upd = _dot(kt_ref[0, 0, :, slb] * k_dec_b, v_ref[0, 0, slb, :])
            sb_ref[...] = upd if cb == nc - 1 else sb_ref[...] * c_dec_b + upd
        contribute(cf, part_f)
        contribute(cb, part_b)


def _retention(q, kt, v, sg, log_decay):
    b, heads, s, dk = q.shape
    dv = v.shape[3]
    chunk = min(RET_CHUNK, s)
    return pl.pallas_call(
        functools.partial(_retention_kernel, chunk),
        grid=(b, heads),
        in_specs=[
            pl.BlockSpec(memory_space=pltpu.SMEM),
            pl.BlockSpec((1, 1, s, dk), lambda i, h: (i, h, 0, 0)),
            pl.BlockSpec((1, 1, dk, s), lambda i, h: (i, h, 0, 0)),
            pl.BlockSpec((1, 1, s, dv), lambda i, h: (i, h, 0, 0)),
            pl.BlockSpec((1, 1, s, dv), lambda i, h: (i, h, 0, 0)),
        ],
        out_specs=pl.BlockSpec((1, 1, s, dv), lambda i, h: (i, h, 0, 0)),
        out_shape=jax.ShapeDtypeStruct((b, heads, s, dv), BF16),
        scratch_shapes=[
            pltpu.VMEM((s, dv), F32),
            pltpu.VMEM((dk, dv), F32),
            pltpu.VMEM((dk, dv), F32),
        ],
        compiler_params=_cparams("parallel", "parallel"),
        name="retention",
    )(log_decay, q, kt, v, sg)


def _ret_out_kernel(o_ref, w_ref, x_ref, gffn_ref, wr_ref, h_ref, hn_ref, l_ref):
    heads, ts = o_ref.shape[1], o_ref.shape[2]
    sub = min(ts, RET_OUT_SUBTILE)
    for r in range(0, ts, sub):
        rows = slice(r, r + sub)
        h = x_ref[0, rows, :]
        for hd in range(heads):
            h = h + _dot(o_ref[0, hd, rows, :], w_ref[hd])
        h_ref[0, rows, :] = h
        _router_epilogue(h, gffn_ref, wr_ref, hn_ref, l_ref, rows)


def _ret_out(o, w_out, x, g_ffn, w_router, l_rows):
    b, heads, s, dv = o.shape
    d = x.shape[2]
    ts = _row_tile(s, 1024)
    tile = lambda i, j: (i, j, 0)
    const = lambda i, j: (0, 0)
    return pl.pallas_call(
        _ret_out_kernel,
        grid=(b, s // ts),
        in_specs=[
            pl.BlockSpec((1, heads, ts, dv), lambda i, j: (i, 0, j, 0)),
            pl.BlockSpec((heads, dv, d), lambda i, j: (0, 0, 0)),
            pl.BlockSpec((1, ts, d), tile),
            pl.BlockSpec((1, d), const),
            pl.BlockSpec(w_router.shape, const),
        ],
        out_specs=[
            pl.BlockSpec((1, ts, d), tile),
            pl.BlockSpec((1, ts, d // 2), tile),
            pl.BlockSpec((1, l_rows, ts), lambda i, j: (i, 0, j)),
        ],
        out_shape=[
            jax.ShapeDtypeStruct((b, s, d), F32),
            jax.ShapeDtypeStruct((b, s, d // 2), I32),
            jax.ShapeDtypeStruct((b, l_rows, s), F32),
        ],
        compiler_params=_cparams("parallel", "parallel"),
        name="ret_out",
    )(o, w_out.reshape(heads, dv, d), x, g_ffn, w_router)


def _exclusive_prefix(flags):
    rows, s = flags.shape
    upper = (lax.broadcasted_iota(I32, (LANES, LANES), 0)
             < lax.broadcasted_iota(I32, (LANES, LANES), 1))
    upper = jnp.where(upper, 1.0, 0.0).astype(BF16)
    carry = jnp.zeros((rows, 1), F32)
    pieces = []
    for k in range(s // LANES):
        blk = flags[:, k * LANES:(k + 1) * LANES]
        pieces.append(_dot(blk.astype(BF16), upper) + carry)
        carry = carry + jnp.sum(blk, axis=1, keepdims=True)
    return jnp.concatenate(pieces, axis=1)


def _router_select_kernel(n_exp, cap, l_ref, rank_ref, gate_ref):
    nb = l_ref.shape[0]
    affs = []
    for i in range(nb):
        l = l_ref[i]
        logits = l[0:n_exp] + l[n_exp:2 * n_exp] + l[2 * n_exp:3 * n_exp]
        e = jnp.exp(logits - jnp.max(logits, axis=0, keepdims=True))
        affs.append(e / jnp.sum(e, axis=0, keepdims=True))
    aff = jnp.concatenate(affs, axis=0)
    keys = pltpu.bitcast(aff, I32)

    def refine(i, thr):
        cand = thr | jnp.left_shift(jnp.int32(1), 30 - i)
        cnt = jnp.sum(jnp.where(keys >= cand, 1.0, 0.0), axis=1, keepdims=True)
        return jnp.where(cnt >= cap, cand, thr)

    thr = lax.fori_loop(0, 31, refine, jnp.zeros((nb * n_exp, 1), I32))
    above = keys > thr
    tied = keys == thr
    n_above = jnp.sum(jnp.where(above, 1.0, 0.0), axis=1, keepdims=True)
    tied_rank = _exclusive_prefix(jnp.where(tied, 1.0, 0.0))
    keep = above | (tied & (tied_rank < cap - n_above))
    rank = _exclusive_prefix(jnp.where(keep, 1.0, 0.0))
    rank = jnp.where(keep, rank.astype(I32), -1)
    gate = jnp.where(keep, aff, 0.0)
    for i in range(nb):
        rank_ref[i] = rank[i * n_exp:(i + 1) * n_exp]
        gate_ref[i] = gate[i * n_exp:(i + 1) * n_exp]


def _router_select(lt, n_exp, cap):
    b, rows, s = lt.shape
    nb = _row_tile(b, 8)
    blk = lambda i: (i, 0, 0)
    return pl.pallas_call(
        functools.partial(_router_select_kernel, n_exp, cap),
        grid=(b // nb,),
        in_specs=[pl.BlockSpec((nb, rows, s), blk)],
        out_specs=[
            pl.BlockSpec((nb, n_exp, s), blk),
            pl.BlockSpec((nb, n_exp, s), blk),
        ],
        out_shape=[
            jax.ShapeDtypeStruct((b, n_exp, s), I32),
            jax.ShapeDtypeStruct((b, n_exp, s), F32),
        ],
        compiler_params=_cparams("parallel"),
        name="router_select",
    )(lt)


def _gather_sc_kernel(n_seq, seq_len, n_exp, e0, ne, cap, win, n_cores, pairs_per_worker,
                      table_hbm, rank_hbm, gate_hbm, zeros_hbm, xe_hbm, gs_hbm,
                      rank_v, gate_v, tok_v, rows_v, gs_v, gsem, osem):
    lanes = SC_LANES
    wid = lax.axis_index("s") * n_cores + lax.axis_index("c")
    lane = lax.broadcasted_iota(I32, (lanes,), 0)
    lane0 = jnp.zeros((lanes,), I32)
    pltpu.sync_copy(zeros_hbm, gs_v)
    n_win = cap // win

    @pl.loop(0, pairs_per_worker)
    def _(i):
        pair = wid * pairs_per_worker + i
        b = pair // ne
        e = pair % ne
        row = b * n_exp + e0 + e
        pltpu.sync_copy(rank_hbm.at[row], rank_v)
        pltpu.sync_copy(gate_hbm.at[row], gate_v)

        @pl.loop(0, seq_len // lanes)
        def _(j):
            r = rank_v[pl.ds(j * lanes, lanes)]
            keep = r >= 0
            slot = jnp.where(keep, r, 0)
            plsc.store_scatter(tok_v, [slot // win, slot % win], lane + (j * lanes + b * seq_len),
                               mask=keep)
            plsc.store_scatter(gs_v, [slot, lane0], gate_v[pl.ds(j * lanes, lanes)], mask=keep)

        out_base = (e * n_seq + b) * cap

        def fetch(w):
            return pltpu.make_async_copy(table_hbm.at[tok_v.at[w]], rows_v.at[w % 2], gsem.at[w % 2])

        def put(w):
            return pltpu.make_async_copy(rows_v.at[w % 2], xe_hbm.at[pl.ds(out_base + w * win, win)],
                                         osem.at[w % 2])

        fetch(0).start()
        for w in range(n_win):
            if w + 1 < n_win:
                if w >= 1:
                    put(w - 1).wait()
                fetch(w + 1).start()
            fetch(w).wait()
            put(w).start()
        if n_win >= 2:
            put(n_win - 2).wait()
        put(n_win - 1).wait()
        pltpu.sync_copy(gs_v, gs_hbm.at[pl.ds(out_base, cap)])


def _gather_sc(table, rank, gate, n_seq, seq_len, n_exp, e0, ne, cap):
    width = table.shape[1]
    info = plsc.get_sparse_core_info()
    n_workers = info.num_cores * info.num_subcores
    pairs = n_seq * ne
    assert info.num_lanes == SC_LANES and pairs % n_workers == 0 and seq_len % SC_LANES == 0
    win = min(cap, SC_GATHER_WINDOW)
    assert cap % win == 0
    mesh = plsc.VectorSubcoreMesh(core_axis_name="c", subcore_axis_name="s")
    call = pl.kernel(
        functools.partial(_gather_sc_kernel, n_seq, seq_len, n_exp, e0, ne, cap, win,
                          info.num_cores, pairs // n_workers),
        mesh=mesh,
        out_type=[jax.ShapeDtypeStruct((ne * n_seq * cap, width), I32),
                  jax.ShapeDtypeStruct((ne * n_seq * cap, LANES), F32)],
        scratch_types=[
            pltpu.VMEM((seq_len,), I32),
            pltpu.VMEM((seq_len,), F32),
            pltpu.VMEM((cap // win, win), I32),
            pltpu.VMEM((2, win, width), I32),
            pltpu.VMEM((cap, LANES), F32),
            pltpu.SemaphoreType.DMA((2,)),
            pltpu.SemaphoreType.DMA((2,)),
        ],
        compiler_params=pltpu.CompilerParams(needs_layout_passes=False),
        name="moe_gather_sc",
    )
    return call(table, rank, gate, jnp.zeros((cap, LANES), F32))


def _ffn_kernel(layer, e0, xe_ref, gs_ref, wg_hbm, wu_hbm, wd_hbm, y_ref,
                wg_s, wu_s, wd_s, stg_g, stg_u, stg_d, sem):
    e = pl.program_id(0)
    m = pl.program_id(1)
    n_exp = pl.num_programs(0)
    nm = pl.num_programs(1)
    n_chunks, d, fc = wg_s.shape[1:]
    cur = e % 2
    par = m % 2

    def chunk_copies(exp, k, buf):
        cols = pl.ds(pl.multiple_of(k * fc, fc), fc)
        return (pltpu.make_async_copy(wg_hbm.at[layer, e0 + exp, :, cols], stg_g.at[buf], sem.at[buf, 0]),
                pltpu.make_async_copy(wu_hbm.at[layer, e0 + exp, :, cols], stg_u.at[buf], sem.at[buf, 1]),
                pltpu.make_async_copy(wd_hbm.at[layer, e0 + exp, cols, :], stg_d.at[buf], sem.at[buf, 2]))

    def convert(buf, slot, k):
        wg_s[slot, k] = stg_g[buf].astype(BF16)
        wu_s[slot, k] = stg_u[buf].astype(BF16)
        wd_s[slot, pl.ds(pl.multiple_of(k * fc, fc), fc), :] = stg_d[buf].astype(BF16)

    first = jnp.logical_and(e == 0, m == 0)

    @pl.when(first)
    def _():
        for c in chunk_copies(0, 0, 0):
            c.start()
        for k in range(n_chunks):
            if k + 1 < n_chunks:
                for c in chunk_copies(0, k + 1, (k + 1) % 2):
                    c.start()
            for c in chunk_copies(0, k, k % 2):
                c.wait()
            if k < n_chunks - 1:
                convert(k % 2, 0, k)

    in_flight = jnp.where(m == 0, e > 0, e < n_exp - 1)

    @pl.when(in_flight)
    def _():
        for c in chunk_copies(0, 0, 1 - par):
            c.wait()

    @pl.when(e < n_exp - 1)
    def _():
        for c in chunk_copies(e + 1, m, par):
            c.start()

    convert(1 - par, jnp.where(m == 0, cur, 1 - cur), jnp.where(m == 0, nm - 1, m - 1))

    x = _unpack_bf16_pairs(xe_ref[0])
    acc = jnp.zeros((x.shape[0], d), F32)
    for k in range(0, n_chunks, 2):
        acts = []
        for kk in (k, k + 1):
            a = _dot(x, wg_s[cur, kk])
            acts.append(a * _sigmoid(a) * _dot(x, wu_s[cur, kk]))
        act = jnp.concatenate(acts, axis=1).astype(BF16)
        acc += _dot(act, wd_s[cur, k * fc:(k + 2) * fc, :])
    y_ref[0] = (acc * gs_ref[0][:, 0:1]).astype(BF16)


def _ffn(layer, e0, xe, gs, wg, wu, wd):
    n_exp, rows, half_d = xe.shape
    d = 2 * half_d
    ff = wg.shape[3]
    fc = min(ff // 2, FFN_WEIGHT_CHUNK)
    nm = ff // fc
    assert ff % fc == 0 and nm % 2 == 0 and rows % nm == 0
    tm = rows // nm
    assert tm % (2 * SUBLANES) == 0
    return pl.pallas_call(
        functools.partial(_ffn_kernel, layer, e0),
        grid=(n_exp, nm),
        in_specs=[
            pl.BlockSpec((1, tm, half_d), lambda e, m: (e, m, 0)),
            pl.BlockSpec((1, tm, LANES), lambda e, m: (e, m, 0)),
            pl.BlockSpec(memory_space=pl.ANY),
            pl.BlockSpec(memory_space=pl.ANY),
            pl.BlockSpec(memory_space=pl.ANY),
        ],
        out_specs=pl.BlockSpec((1, tm, d), lambda e, m: (e, m, 0)),
        out_shape=jax.ShapeDtypeStruct((n_exp, rows, d), BF16),
        scratch_shapes=[
            pltpu.VMEM((2, nm, d, fc), BF16),
            pltpu.VMEM((2, nm, d, fc), BF16),
            pltpu.VMEM((2, ff, d), BF16),
            pltpu.VMEM((2, d, fc), F32),
            pltpu.VMEM((2, d, fc), F32),
            pltpu.VMEM((2, fc, d), F32),
            pltpu.SemaphoreType.DMA((2, 3)),
        ],
        compiler_params=_cparams("arbitrary", "arbitrary"),
        name="moe_ffn",
    )(xe, gs, wg, wu, wd)


def _combine_ple_kernel(final, n_groups, h_ref, *refs):
    y_refs = refs[:n_groups]
    rankt_ref, p_ref, gple_ref, wgate_ref, wproj_ref, gfin_ref, o_ref = refs[n_groups:]
    _, _, cap, d = y_refs[0].shape
    ts = h_ref.shape[1]
    rt = rankt_ref[0]
    slot = lax.broadcasted_iota(I32, (ts, cap), 1)
    h = h_ref[0]
    e0 = 0
    for y_ref in y_refs:
        ne = y_ref.shape[0]
        onehot = jnp.concatenate(
            [jnp.where(rt[:, e:e + 1] == slot, 1.0, 0.0).astype(BF16) for e in range(e0, e0 + ne)],
            axis=1)
        h = h + _dot(onehot, y_ref[:, 0].reshape(ne * cap, d))
        e0 += ne
    gate = _sigmoid(_dot(_rms(h, gple_ref[...]).astype(BF16), wgate_ref[...]))
    h = h + gate * _dot(p_ref[...].astype(BF16), wproj_ref[...])
    if final:
        h = _rms(h, gfin_ref[...])
    o_ref[0] = h


def _combine_ple(layer, b0, h, ys, rankt, p, g_ple, w_gate, w_proj, g_final, final):
    b, s, d = h.shape
    cap = ys[0].shape[2]
    n_exp = sum(y.shape[0] for y in ys)
    ple = p.shape[3]
    ts = _row_tile(s, 1024)
    const = lambda i, j: (0, 0)
    return pl.pallas_call(
        functools.partial(_combine_ple_kernel, final, len(ys)),
        grid=(b, s // ts),
        in_specs=[
            pl.BlockSpec((1, ts, d), lambda i, j: (i, j, 0)),
            *[pl.BlockSpec((y.shape[0], 1, cap, d), lambda i, j: (0, i, 0, 0)) for y in ys],
            pl.BlockSpec((1, ts, n_exp), lambda i, j: (i, j, 0)),
            pl.BlockSpec((None, None, ts, ple), lambda i, j: (layer, b0 + i, j, 0)),
            pl.BlockSpec((1, d), const),
            pl.BlockSpec((d, d), const),
            pl.BlockSpec((ple, d), const),
            pl.BlockSpec((1, d), const),
        ],
        out_specs=pl.BlockSpec((1, ts, d), lambda i, j: (i, j, 0)),
        out_shape=jax.ShapeDtypeStruct((b, s, d), F32),
        compiler_params=_cparams("parallel", "arbitrary"),
        name="combine_ple",
    )(h, *ys, rankt, p, g_ple, w_gate, w_proj, g_final)


def _split_router_weight(w):
    d, n_exp = w.shape
    hi = w.astype(BF16)
    r1 = w - hi.astype(F32)
    mid = r1.astype(BF16)
    lo = (r1 - mid.astype(F32)).astype(BF16)
    pad = jnp.zeros((d, LANES - 3 * n_exp), BF16)
    return jnp.concatenate([hi, mid, lo, pad], axis=1)


def _moe_ple(layer, b0, h, hn, lt, p, n_exp, w_gate, w_up, w_down, g_ple, ple_w_gate, ple_w_proj,
             g_final, final):
    b, s, d = h.shape
    cap = EC_CAPACITY_FACTOR * s // n_exp
    rank, gate = _router_select(lt, n_exp, cap)
    split = n_exp - EXPERT_GROUP_SPLIT if 0 < EXPERT_GROUP_SPLIT < n_exp else n_exp
    groups = [(0, split)] + ([(split, n_exp - split)] if split < n_exp else [])
    table = hn.reshape(b * s, d // 2)
    rank2 = rank.reshape(b * n_exp, s)
    gate2 = gate.reshape(b * n_exp, s)
    gathered = [_gather_sc(table, rank2, gate2, b, s, n_exp, e0, ne, cap) for e0, ne in groups]
    ys = []
    for (e0, ne), (xe, gs) in zip(groups, gathered):
        y = _ffn(layer, e0, xe.reshape(ne, b * cap, d // 2), gs.reshape(ne, b * cap, LANES),
                 w_gate, w_up, w_down)
        ys.append(y.reshape(ne, b, cap, d))
    rankt = jnp.swapaxes(rank, 1, 2)
    return _combine_ple(layer, b0, h, ys, rankt, p, g_ple, ple_w_gate, ple_w_proj, g_final, final)


def _rope_tables(s, half):
    inv = ROPE_BASE ** (-jnp.arange(half, dtype=F32) / half)
    ang = jnp.arange(s).astype(F32)[:, None] * inv[None, :]
    return jnp.cos(ang), jnp.sin(ang)


def _forward(x, b0, p, norm_mix, norm_ffn, norm_ple, final_norm, conv_w_in, conv_w, conv_b,
             conv_w_out, ret_w_in, ret_log_decay, ret_w_out, router_w, exp_w_gate, exp_w_up,
             exp_w_down, ple_w_proj, ple_w_gate):
    b, s, d = x.shape
    depth = p.shape[0]
    heads = ret_log_decay.shape[2]
    n_exp = router_w.shape[2]
    hq = d
    hv = (ret_w_in.shape[2] - 2 * hq) // 2
    h = x
    for i in range(depth):
        j = i // 2
        g_mix = norm_mix[i][None, :]
        g_ffn = norm_ffn[i][None, :]
        w_router = _split_router_weight(router_w[i])
        if i % 2 == 0:
            h, hn, lt = _conv_mixer(h, g_mix, conv_w_in[j].astype(BF16), conv_w[j],
                                      conv_b[j][None, :], conv_w_out[j].astype(BF16),
                                      g_ffn, w_router, 3 * n_exp)
        else:
            w_in = ret_w_in[j]
            wq = w_in[:, :hq].astype(BF16)
            wkt = w_in[:, hq:2 * hq].T.astype(BF16)
            wv = w_in[:, 2 * hq:2 * hq + hv].astype(BF16)
            wg = w_in[:, 2 * hq + hv:].astype(BF16)
            cos, sin = _rope_tables(s, hq // heads // 2)
            q, kt, v, sg = _ret_in(h, g_mix, wq, wkt, wv, wg, cos, sin, heads)
            o = _retention(q, kt, v, sg, ret_log_decay[j])
            h, hn, lt = _ret_out(o, ret_w_out[j].astype(BF16), h, g_ffn, w_router, 3 * n_exp)
        h = _moe_ple(i, b0, h, hn, lt, p, n_exp, exp_w_gate, exp_w_up, exp_w_down,
                     norm_ple[i][None, :],
                     ple_w_gate[i].astype(BF16), ple_w_proj[i].astype(BF16),
                     final_norm[None, :], i == depth - 1)
    return h


def kernel(x, p, norm_mix, norm_ffn, norm_ple, final_norm, conv_w_in, conv_w, conv_b, conv_w_out,
           ret_w_in, ret_log_decay, ret_w_out, router_w, exp_w_gate, exp_w_up, exp_w_down,
           ple_w_proj, ple_w_gate):
    return _forward(x, 0, p, norm_mix, norm_ffn, norm_ple, final_norm, conv_w_in, conv_w, conv_b,
                    conv_w_out, ret_w_in, ret_log_decay, ret_w_out, router_w, exp_w_gate, exp_w_up,
                    exp_w_down, ple_w_proj, ple_w_gate)
```

```python
import functools

import jax
import jax.numpy as jnp
from jax import lax
from jax.experimental import pallas as pl
from jax.experimental.pallas import tpu as pltpu
from jax.experimental.pallas import tpu_sc as plsc

F32 = jnp.float32
BF16 = jnp.bfloat16
I32 = jnp.int32

NORM_EPS = 1e-6
GN_EPS = 1e-5
ROPE_BASE = 10000.0
EC_CAPACITY_FACTOR = 2
RET_CHUNK = 256
LANES = 128
SUBLANES = 8
SC_LANES = 16
SC_GATHER_WINDOW = 64
EXPERT_GROUP_SPLIT = 4
RET_IN_SUBTILE = 256
RET_OUT_SUBTILE = 256
FFN_WEIGHT_CHUNK = 256
VMEM_LIMIT = 56 * 1024 * 1024


def _cparams(*sem):
    return pltpu.CompilerParams(dimension_semantics=sem, vmem_limit_bytes=VMEM_LIMIT)


def _row_tile(n, target):
    t = min(n, target)
    while n % t:
        t //= 2
    return t


def _rms(x, g):
    return x * lax.rsqrt(jnp.mean(x * x, axis=-1, keepdims=True) + NORM_EPS) * g


def _sigmoid(x):
    return 1.0 / (1.0 + jnp.exp(-x))


def _dot(a, b):
    return jnp.dot(a, b, preferred_element_type=F32)


def _pack_bf16_pairs(x):
    w = x.shape[1] // 2
    lo = lax.bitcast_convert_type(x[:, :w].astype(BF16).astype(F32), I32)
    hi = lax.bitcast_convert_type(x[:, w:].astype(BF16).astype(F32), I32)
    return lax.shift_right_logical(lo, jnp.int32(16)) | (hi & jnp.int32(-65536))


def _unpack_bf16_pairs(p):
    lo = lax.bitcast_convert_type(lax.shift_left(p, jnp.int32(16)), F32).astype(BF16)
    hi = lax.bitcast_convert_type(p & jnp.int32(-65536), F32).astype(BF16)
    return jnp.concatenate([lo, hi], axis=1)


def _router_epilogue(h, g_ref, wr_ref, hn_ref, l_ref, rows=slice(None)):
    hn = _rms(h, g_ref[...])
    hi = hn.astype(BF16)
    lo = (hn - hi.astype(F32)).astype(BF16)
    hn_ref[0, rows, :] = _pack_bf16_pairs(hn)
    logits = _dot(hi, wr_ref[...]) + _dot(lo, wr_ref[...])
    l_ref[0, :, rows] = logits.T[:l_ref.shape[1], :]


def _conv_mixer_kernel(xc_ref, xp_ref, g_ref, win_ref, cw_ref, cb_ref, wout_ref, gffn_ref, wr_ref,
                       o_ref, hn_ref, l_ref, u_ref, bg_ref, edge_ref):
    j = pl.program_id(1)
    nj = pl.num_programs(1) - 1
    ts, d = xc_ref.shape[1], xc_ref.shape[2]
    slot = j % 2

    def project():
        hn = _rms(xc_ref[0], g_ref[...]).astype(BF16)
        bg_ref[slot] = _dot(hn, win_ref[:, :d]).astype(BF16)
        u_new = _dot(hn, win_ref[:, d:2 * d]) * _dot(hn, win_ref[:, 2 * d:])
        u_ref[slot] = u_new
        return u_new[0:1, :]

    def finish(next_row):
        u = u_ref[1 - slot]
        prev_row = edge_ref[0:1, :]
        row = lax.broadcasted_iota(I32, u.shape, 0)
        u_prev = jnp.where(row == 0, prev_row, pltpu.roll(u, 1, axis=0))
        u_next = jnp.where(row == ts - 1, next_row, pltpu.roll(u, ts - 1, axis=0))
        cw = cw_ref[...]
        edge_ref[0:1, :] = u[ts - 1:ts, :]
        sub = ts // 2
        for r in range(0, ts, sub):
            rows = slice(r, r + sub)
            y = (cw[0:1, :] * u_prev[rows] + cw[1:2, :] * u[rows] + cw[2:3, :] * u_next[rows]
                 + cb_ref[...])
            z = (bg_ref[1 - slot, rows, :].astype(F32) * y).astype(BF16)
            h = xp_ref[0, rows, :] + _dot(z, wout_ref[...])
            o_ref[0, rows, :] = h
            _router_epilogue(h, gffn_ref, wr_ref, hn_ref, l_ref, rows)

    @pl.when(j == 0)
    def _():
        edge_ref[...] = jnp.zeros_like(edge_ref)
        project()

    @pl.when(jnp.logical_and(j > 0, j < nj))
    def _():
        finish(project())

    @pl.when(j == nj)
    def _():
        finish(jnp.zeros((1, d), F32))


def _conv_mixer(x, g, w_in, conv_w, conv_b, w_out, g_ffn, w_router, l_rows):
    b, s, d = x.shape
    ts = _row_tile(s, 512)
    nj = s // ts
    cur = lambda i, j: (i, jnp.minimum(j, nj - 1), 0)
    prev = lambda i, j: (i, jnp.maximum(j - 1, 0), 0)
    const = lambda i, j: (0, 0)
    return pl.pallas_call(
        _conv_mixer_kernel,
        grid=(b, nj + 1),
        in_specs=[
            pl.BlockSpec((1, ts, d), cur),
            pl.BlockSpec((1, ts, d), prev),
            pl.BlockSpec((1, d), const),
            pl.BlockSpec((d, 3 * d), const),
            pl.BlockSpec((3, d), const),
            pl.BlockSpec((1, d), const),
            pl.BlockSpec((d, d), const),
            pl.BlockSpec((1, d), const),
            pl.BlockSpec(w_router.shape, const),
        ],
        out_specs=[
            pl.BlockSpec((1, ts, d), prev),
            pl.BlockSpec((1, ts, d // 2), prev),
            pl.BlockSpec((1, l_rows, ts), lambda i, j: (i, 0, jnp.maximum(j - 1, 0))),
        ],
        out_shape=[
            jax.ShapeDtypeStruct((b, s, d), F32),
            jax.ShapeDtypeStruct((b, s, d // 2), I32),
            jax.ShapeDtypeStruct((b, l_rows, s), F32),
        ],
        scratch_shapes=[
            pltpu.VMEM((2, ts, d), F32),
            pltpu.VMEM((2, ts, d), BF16),
            pltpu.VMEM((SUBLANES, d), F32),
        ],
        compiler_params=_cparams("parallel", "arbitrary"),
        name="conv_mixer",
    )(x, x, g, w_in, conv_w, conv_b, w_out, g_ffn, w_router)


def _ret_in_kernel(x_ref, g_ref, wq_ref, wkt_ref, wv_ref, wg_ref,
                   cos_ref, sin_ref, cost_ref, sint_ref,
                   q_ref, kt_ref, v_ref, sg_ref):
    heads, tm, dk = q_ref.shape[1:]
    dv = v_ref.shape[3]
    half = dk // 2
    scale = dk ** -0.5
    sub = min(tm, RET_IN_SUBTILE)
    for r in range(0, tm, sub):
        rows = slice(r, r + sub)
        hn = _rms(x_ref[0, rows, :], g_ref[...]).astype(BF16)
        q = _dot(hn, wq_ref[...])
        cos = cos_ref[rows, :]
        sin = sin_ref[rows, :]
        for h in range(heads):
            x1 = q[:, h * dk:h * dk + half]
            x2 = q[:, h * dk + half:(h + 1) * dk]
            q_ref[0, h, rows, :half] = (x1 * cos - x2 * sin).astype(BF16)
            q_ref[0, h, rows, half:] = (x1 * sin + x2 * cos).astype(BF16)
        kt = lax.dot_general(wkt_ref[...], hn, (((1,), (1,)), ((), ())),
                             preferred_element_type=F32)
        cost = cost_ref[:, rows]
        sint = sint_ref[:, rows]
        for h in range(heads):
            x1 = kt[h * dk:h * dk + half, :]
            x2 = kt[h * dk + half:(h + 1) * dk, :]
            kt_ref[0, h, :half, rows] = ((x1 * cost - x2 * sint) * scale).astype(BF16)
            kt_ref[0, h, half:, rows] = ((x1 * sint + x2 * cost) * scale).astype(BF16)
        v = _dot(hn, wv_ref[...])
        gate = _dot(hn, wg_ref[...])
        sg = gate * _sigmoid(gate)
        for h in range(heads):
            v_ref[0, h, rows, :] = v[:, h * dv:(h + 1) * dv].astype(BF16)
            sg_ref[0, h, rows, :] = sg[:, h * dv:(h + 1) * dv].astype(BF16)


def _ret_in(x, g, wq, wkt, wv, wg, cos, sin, heads):
    b, s, d = x.shape
    hq = wq.shape[1]
    hv = wv.shape[1]
    dk = hq // heads
    dv = hv // heads
    half = cos.shape[1]
    tm = _row_tile(s, 1024)
    cost = cos.T
    sint = sin.T
    const = lambda i, j: (0, 0)
    return pl.pallas_call(
        _ret_in_kernel,
        grid=(b, s // tm),
        in_specs=[
            pl.BlockSpec((1, tm, d), lambda i, j: (i, j, 0)),
            pl.BlockSpec((1, d), const),
            pl.BlockSpec((d, hq), const, pipeline_mode=pl.Buffered(1)),
            pl.BlockSpec((hq, d), const, pipeline_mode=pl.Buffered(1)),
            pl.BlockSpec((d, hv), const, pipeline_mode=pl.Buffered(1)),
            pl.BlockSpec((d, hv), const, pipeline_mode=pl.Buffered(1)),
            pl.BlockSpec((tm, half), lambda i, j: (j, 0)),
            pl.BlockSpec((tm, half), lambda i, j: (j, 0)),
            pl.BlockSpec((half, tm), lambda i, j: (0, j)),
            pl.BlockSpec((half, tm), lambda i, j: (0, j)),
        ],
        out_specs=[
            pl.BlockSpec((1, heads, tm, dk), lambda i, j: (i, 0, j, 0)),
            pl.BlockSpec((1, heads, dk, tm), lambda i, j: (i, 0, 0, j)),
            pl.BlockSpec((1, heads, tm, dv), lambda i, j: (i, 0, j, 0)),
            pl.BlockSpec((1, heads, tm, dv), lambda i, j: (i, 0, j, 0)),
        ],
        out_shape=[
            jax.ShapeDtypeStruct((b, heads, s, dk), BF16),
            jax.ShapeDtypeStruct((b, heads, dk, s), BF16),
            jax.ShapeDtypeStruct((b, heads, s, dv), BF16),
            jax.ShapeDtypeStruct((b, heads, s, dv), BF16),
        ],
        compiler_params=_cparams("parallel", "parallel"),
        name="ret_in",
    )(x, g, wq, wkt, wv, wg, cos, sin, cost, sint)


def _retention_kernel(chunk, ld_ref, q_ref, kt_ref, v_ref, sg_ref, o_ref, acc_ref, sf_ref, sb_ref):
    h = pl.program_id(1)
    s = q_ref.shape[2]
    L = chunk
    nc = s // L
    lgf = ld_ref[0, h]
    lgb = ld_ref[1, h]
    ii = lax.broadcasted_iota(I32, (L, L), 0)
    jj = lax.broadcasted_iota(I32, (L, L), 1)
    dist = (ii - jj).astype(F32)
    causal = ii >= jj
    decay = jnp.where(causal,
                      jnp.exp(lgf * jnp.where(causal, dist, 0.0)),
                      jnp.exp(lgb * jnp.where(causal, 0.0, -dist)))
    col = lax.broadcasted_iota(I32, (L, 1), 0).astype(F32)
    row = lax.broadcasted_iota(I32, (1, L), 1).astype(F32)
    q_dec_f = jnp.exp(lgf * (col + 1.0)).astype(BF16)
    q_dec_b = jnp.exp(lgb * (L - col)).astype(BF16)
    k_dec_f = jnp.exp(lgf * (L - 1.0 - row)).astype(BF16)
    k_dec_b = jnp.exp(lgb * row).astype(BF16)
    one = jnp.ones((1, 1), F32)
    c_dec_f = jnp.exp(lgf * L * one)
    c_dec_b = jnp.exp(lgb * L * one)

    def finish(c, out):
        sl = slice(c * L, (c + 1) * L)
        mu = jnp.mean(out, axis=-1, keepdims=True)
        cen = out - mu
        var = jnp.mean(cen * cen, axis=-1, keepdims=True)
        o_ref[0, 0, sl, :] = (sg_ref[0, 0, sl, :].astype(F32) * (cen * lax.rsqrt(var + GN_EPS))
                              ).astype(BF16)

    stored = {}

    def contribute(c, part):
        sl = slice(c * L, (c + 1) * L)
        if c not in stored:
            stored[c] = part is not None
            if part is not None:
                acc_ref[sl, :] = part
        elif not stored[c]:
            finish(c, part)
        else:
            finish(c, acc_ref[sl, :] if part is None else acc_ref[sl, :] + part)

    for k in range(nc):
        cf, cb = k, nc - 1 - k
        slf = slice(cf * L, (cf + 1) * L)
        slb = slice(cb * L, (cb + 1) * L)
        qc, ktc, vc = q_ref[0, 0, slf, :], kt_ref[0, 0, :, slf], v_ref[0, 0, slf, :]
        scores = (_dot(qc, ktc) * decay).astype(BF16)
        part_f = _dot(scores, vc)
        if cf > 0:
            part_f += _dot(qc * q_dec_f, sf_ref[...].astype(BF16))
        if cf < nc - 1:
            upd = _dot(ktc * k_dec_f, vc)
            sf_ref[...] = upd if cf == 0 else sf_ref[...] * c_dec_f + upd
        part_b = None
        if cb < nc - 1:
            part_b = _dot(q_ref[0, 0, slb, :] * q_dec_b, sb_ref[...].astype(BF16))
        if cb > 0:
            upd = _dot(kt_ref[0, 0, :, slb] * k_dec_b, v_ref[0, 0, slb, :])
            sb_ref[...] = upd if cb == nc - 1 else sb_ref[...] * c_dec_b + upd
        contribute(cf, part_f)
        contribute(cb, part_b)


def _retention(q, kt, v, sg, log_decay):
    b, heads, s, dk = q.shape
    dv = v.shape[3]
    chunk = min(RET_CHUNK, s)
    return pl.pallas_call(
        functools.partial(_retention_kernel, chunk),
        grid=(b, heads),
        in_specs=[
            pl.BlockSpec(memory_space=pltpu.SMEM),
            pl.BlockSpec((1, 1, s, dk), lambda i, h: (i, h, 0, 0)),
            pl.BlockSpec((1, 1, dk, s), lambda i, h: (i, h, 0, 0)),
            pl.BlockSpec((1, 1, s, dv), lambda i, h: (i, h, 0, 0)),
            pl.BlockSpec((1, 1, s, dv), lambda i, h: (i, h, 0, 0)),
        ],
        out_specs=pl.BlockSpec((1, 1, s, dv), lambda i, h: (i, h, 0, 0)),
        out_shape=jax.ShapeDtypeStruct((b, heads, s, dv), BF16),
        scratch_shapes=[
            pltpu.VMEM((s, dv), F32),
            pltpu.VMEM((dk, dv), F32),
            pltpu.VMEM((dk, dv), F32),
        ],
        compiler_params=_cparams("parallel", "parallel"),
        name="retention",
    )(log_decay, q, kt, v, sg)


def _ret_out_kernel(o_ref, w_ref, x_ref, gffn_ref, wr_ref, h_ref, hn_ref, l_ref):
    heads, ts = o_ref.shape[1], o_ref.shape[2]
    sub = min(ts, RET_OUT_SUBTILE)
    for r in range(0, ts, sub):
        rows = slice(r, r + sub)
        h = x_ref[0, rows, :]
        for hd in range(heads):
            h = h + _dot(o_ref[0, hd, rows, :], w_ref[hd])
        h_ref[0, rows, :] = h
        _router_epilogue(h, gffn_ref, wr_ref, hn_ref, l_ref, rows)


def _ret_out(o, w_out, x, g_ffn, w_router, l_rows):
    b, heads, s, dv = o.shape
    d = x.shape[2]
    ts = _row_tile(s, 1024)
    tile = lambda i, j: (i, j, 0)
    const = lambda i, j: (0, 0)
    return pl.pallas_call(
        _ret_out_kernel,
        grid=(b, s // ts),
        in_specs=[
            pl.BlockSpec((1, heads, ts, dv), lambda i, j: (i, 0, j, 0)),
            pl.BlockSpec((heads, dv, d), lambda i, j: (0, 0, 0)),
            pl.BlockSpec((1, ts, d), tile),
            pl.BlockSpec((1, d), const),
            pl.BlockSpec(w_router.shape, const),
        ],
        out_specs=[
            pl.BlockSpec((1, ts, d), tile),
            pl.BlockSpec((1, ts, d // 2), tile),
            pl.BlockSpec((1, l_rows, ts), lambda i, j: (i, 0, j)),
        ],
        out_shape=[
            jax.ShapeDtypeStruct((b, s, d), F32),
            jax.ShapeDtypeStruct((b, s, d // 2), I32),
            jax.ShapeDtypeStruct((b, l_rows, s), F32),
        ],
        compiler_params=_cparams("parallel", "parallel"),
        name="ret_out",
    )(o, w_out.reshape(heads, dv, d), x, g_ffn, w_router)


def _exclusive_prefix(flags):
    rows, s = flags.shape
    upper = (lax.broadcasted_iota(I32, (LANES, LANES), 0)
             < lax.broadcasted_iota(I32, (LANES, LANES), 1))
    upper = jnp.where(upper, 1.0, 0.0).astype(BF16)
    carry = jnp.zeros((rows, 1), F32)
    pieces = []
    for k in range(s // LANES):
        blk = flags[:, k * LANES:(k + 1) * LANES]
        pieces.append(_dot(blk.astype(BF16), upper) + carry)
        carry = carry + jnp.sum(blk, axis=1, keepdims=True)
    return jnp.concatenate(pieces, axis=1)


def _router_select_kernel(n_exp, cap, l_ref, rank_ref, gate_ref):
    nb = l_ref.shape[0]
    affs = []
    for i in range(nb):
        l = l_ref[i]
        logits = l[0:n_exp] + l[n_exp:2 * n_exp] + l[2 * n_exp:3 * n_exp]
        e = jnp.exp(logits - jnp.max(logits, axis=0, keepdims=True))
        affs.append(e / jnp.sum(e, axis=0, keepdims=True))
    aff = jnp.concatenate(affs, axis=0)
    keys = pltpu.bitcast(aff, I32)

    def refine(i, thr):
        cand = thr | jnp.left_shift(jnp.int32(1), 30 - i)
        cnt = jnp.sum(jnp.where(keys >= cand, 1.0, 0.0), axis=1, keepdims=True)
        return jnp.where(cnt >= cap, cand, thr)

    thr = lax.fori_loop(0, 31, refine, jnp.zeros((nb * n_exp, 1), I32))
    above = keys > thr
    tied = keys == thr
    n_above = jnp.sum(jnp.where(above, 1.0, 0.0), axis=1, keepdims=True)
    tied_rank = _exclusive_prefix(jnp.where(tied, 1.0, 0.0))
    keep = above | (tied & (tied_rank < cap - n_above))
    rank = _exclusive_prefix(jnp.where(keep, 1.0, 0.0))
    rank = jnp.where(keep, rank.astype(I32), -1)
    gate = jnp.where(keep, aff, 0.0)
    for i in range(nb):
        rank_ref[i] = rank[i * n_exp:(i + 1) * n_exp]
        gate_ref[i] = gate[i * n_exp:(i + 1) * n_exp]


def _router_select(lt, n_exp, cap):
    b, rows, s = lt.shape
    nb = _row_tile(b, 8)
    blk = lambda i: (i, 0, 0)
    return pl.pallas_call(
        functools.partial(_router_select_kernel, n_exp, cap),
        grid=(b // nb,),
        in_specs=[pl.BlockSpec((nb, rows, s), blk)],
        out_specs=[
            pl.BlockSpec((nb, n_exp, s), blk),
            pl.BlockSpec((nb, n_exp, s), blk),
        ],
        out_shape=[
            jax.ShapeDtypeStruct((b, n_exp, s), I32),
            jax.ShapeDtypeStruct((b, n_exp, s), F32),
        ],
        compiler_params=_cparams("parallel"),
        name="router_select",
    )(lt)


def _gather_sc_kernel(n_seq, seq_len, n_exp, e0, ne, cap, win, n_cores, pairs_per_worker,
                      table_hbm, rank_hbm, gate_hbm, zeros_hbm, xe_hbm, gs_hbm,
                      rank_v, gate_v, tok_v, rows_v, gs_v, gsem, osem):
    lanes = SC_LANES
    wid = lax.axis_index("s") * n_cores + lax.axis_index("c")
    lane = lax.broadcasted_iota(I32, (lanes,), 0)
    lane0 = jnp.zeros((lanes,), I32)
    pltpu.sync_copy(zeros_hbm, gs_v)
    n_win = cap // win

    @pl.loop(0, pairs_per_worker)
    def _(i):
        pair = wid * pairs_per_worker + i
        b = pair // ne
        e = pair % ne
        row = b * n_exp + e0 + e
        pltpu.sync_copy(rank_hbm.at[row], rank_v)
        pltpu.sync_copy(gate_hbm.at[row], gate_v)

        @pl.loop(0, seq_len // lanes)
        def _(j):
            r = rank_v[pl.ds(j * lanes, lanes)]
            keep = r >= 0
            slot = jnp.where(keep, r, 0)
            plsc.store_scatter(tok_v, [slot // win, slot % win], lane + (j * lanes + b * seq_len),
                               mask=keep)
            plsc.store_scatter(gs_v, [slot, lane0], gate_v[pl.ds(j * lanes, lanes)], mask=keep)

        out_base = (e * n_seq + b) * cap

        def fetch(w):
            return pltpu.make_async_copy(table_hbm.at[tok_v.at[w]], rows_v.at[w % 2], gsem.at[w % 2])

        def put(w):
            return pltpu.make_async_copy(rows_v.at[w % 2], xe_hbm.at[pl.ds(out_base + w * win, win)],
                                         osem.at[w % 2])

        fetch(0).start()
        for w in range(n_win):
            if w + 1 < n_win:
                if w >= 1:
                    put(w - 1).wait()
                fetch(w + 1).start()
            fetch(w).wait()
            put(w).start()
        if n_win >= 2:
            put(n_win - 2).wait()
        put(n_win - 1).wait()
        pltpu.sync_copy(gs_v, gs_hbm.at[pl.ds(out_base, cap)])


def _gather_sc(table, rank, gate, n_seq, seq_len, n_exp, e0, ne, cap):
    width = table.shape[1]
    info = plsc.get_sparse_core_info()
    n_workers = info.num_cores * info.num_subcores
    pairs = n_seq * ne
    assert info.num_lanes == SC_LANES and pairs % n_workers == 0 and seq_len % SC_LANES == 0
    win = min(cap, SC_GATHER_WINDOW)
    assert cap % win == 0
    mesh = plsc.VectorSubcoreMesh(core_axis_name="c", subcore_axis_name="s")
    call = pl.kernel(
        functools.partial(_gather_sc_kernel, n_seq, seq_len, n_exp, e0, ne, cap, win,
                          info.num_cores, pairs // n_workers),
        mesh=mesh,
        out_type=[jax.ShapeDtypeStruct((ne * n_seq * cap, width), I32),
                  jax.ShapeDtypeStruct((ne * n_seq * cap, LANES), F32)],
        scratch_types=[
            pltpu.VMEM((seq_len,), I32),
            pltpu.VMEM((seq_len,), F32),
            pltpu.VMEM((cap // win, win), I32),
            pltpu.VMEM((2, win, width), I32),
            pltpu.VMEM((cap, LANES), F32),
            pltpu.SemaphoreType.DMA((2,)),
            pltpu.SemaphoreType.DMA((2,)),
        ],
        compiler_params=pltpu.CompilerParams(needs_layout_passes=False),
        name="moe_gather_sc",
    )
    return call(table, rank, gate, jnp.zeros((cap, LANES), F32))


def _ffn_kernel(layer, e0, xe_ref, gs_ref, wg_hbm, wu_hbm, wd_hbm, y_ref,
                wg_s, wu_s, wd_s, stg_g, stg_u, stg_d, sem):
    e = pl.program_id(0)
    m = pl.program_id(1)
    n_exp = pl.num_programs(0)
    nm = pl.num_programs(1)
    n_chunks, d, fc = wg_s.shape[1:]
    cur = e % 2
    par = m % 2

    def chunk_copies(exp, k, buf):
        cols = pl.ds(pl.multiple_of(k * fc, fc), fc)
        return (pltpu.make_async_copy(wg_hbm.at[layer, e0 + exp, :, cols], stg_g.at[buf], sem.at[buf, 0]),
                pltpu.make_async_copy(wu_hbm.at[layer, e0 + exp, :, cols], stg_u.at[buf], sem.at[buf, 1]),
                pltpu.make_async_copy(wd_hbm.at[layer, e0 + exp, cols, :], stg_d.at[buf], sem.at[buf, 2]))

    def convert(buf, slot, k):
        wg_s[slot, k] = stg_g[buf].astype(BF16)
        wu_s[slot, k] = stg_u[buf].astype(BF16)
        wd_s[slot, pl.ds(pl.multiple_of(k * fc, fc), fc), :] = stg_d[buf].astype(BF16)

    first = jnp.logical_and(e == 0, m == 0)

    @pl.when(first)
    def _():
        for c in chunk_copies(0, 0, 0):
            c.start()
        for k in range(n_chunks):
            if k + 1 < n_chunks:
                for c in chunk_copies(0, k + 1, (k + 1) % 2):
                    c.start()
            for c in chunk_copies(0, k, k % 2):
                c.wait()
            if k < n_chunks - 1:
                convert(k % 2, 0, k)

    in_flight = jnp.where(m == 0, e > 0, e < n_exp - 1)

    @pl.when(in_flight)
    def _():
        for c in chunk_copies(0, 0, 1 - par):
            c.wait()

    @pl.when(e < n_exp - 1)
    def _():
        for c in chunk_copies(e + 1, m, par):
            c.start()

    convert(1 - par, jnp.where(m == 0, cur, 1 - cur), jnp.where(m == 0, nm - 1, m - 1))

    x = _unpack_bf16_pairs(xe_ref[0])
    acc = jnp.zeros((x.shape[0], d), F32)
    for k in range(0, n_chunks, 2):
        acts = []
        for kk in (k, k + 1):
            a = _dot(x, wg_s[cur, kk])
            acts.append(a * _sigmoid(a) * _dot(x, wu_s[cur, kk]))
        act = jnp.concatenate(acts, axis=1).astype(BF16)
        acc += _dot(act, wd_s[cur, k * fc:(k + 2) * fc, :])
    y_ref[0] = (acc * gs_ref[0][:, 0:1]).astype(BF16)


def _ffn(layer, e0, xe, gs, wg, wu, wd):
    n_exp, rows, half_d = xe.shape
    d = 2 * half_d
    ff = wg.shape[3]
    fc = min(ff // 2, FFN_WEIGHT_CHUNK)
    nm = ff // fc
    assert ff % fc == 0 and nm % 2 == 0 and rows % nm == 0
    tm = rows // nm
    assert tm % (2 * SUBLANES) == 0
    return pl.pallas_call(
        functools.partial(_ffn_kernel, layer, e0),
        grid=(n_exp, nm),
        in_specs=[
            pl.BlockSpec((1, tm, half_d), lambda e, m: (e, m, 0)),
            pl.BlockSpec((1, tm, LANES), lambda e, m: (e, m, 0)),
            pl.BlockSpec(memory_space=pl.ANY),
            pl.BlockSpec(memory_space=pl.ANY),
            pl.BlockSpec(memory_space=pl.ANY),
        ],
        out_specs=pl.BlockSpec((1, tm, d), lambda e, m: (e, m, 0)),
        out_shape=jax.ShapeDtypeStruct((n_exp, rows, d), BF16),
        scratch_shapes=[
            pltpu.VMEM((2, nm, d, fc), BF16),
            pltpu.VMEM((2, nm, d, fc), BF16),
            pltpu.VMEM((2, ff, d), BF16),
            pltpu.VMEM((2, d, fc), F32),
            pltpu.VMEM((2, d, fc), F32),
            pltpu.VMEM((2, fc, d), F32),
            pltpu.SemaphoreType.DMA((2, 3)),
        ],
        compiler_params=_cparams("arbitrary", "arbitrary"),
        name="moe_ffn",
    )(xe, gs, wg, wu, wd)


def _combine_ple_kernel(final, n_groups, h_ref, *refs):
    y_refs = refs[:n_groups]
    rankt_ref, p_ref, gple_ref, wgate_ref, wproj_ref, gfin_ref, o_ref = refs[n_groups:]
    _, _, cap, d = y_refs[0].shape
    ts = h_ref.shape[1]
    rt = rankt_ref[0]
    slot = lax.broadcasted_iota(I32, (ts, cap), 1)
    h = h_ref[0]
    e0 = 0
    for y_ref in y_refs:
        ne = y_ref.shape[0]
        onehot = jnp.concatenate(
            [jnp.where(rt[:, e:e + 1] == slot, 1.0, 0.0).astype(BF16) for e in range(e0, e0 + ne)],
            axis=1)
        h = h + _dot(onehot, y_ref[:, 0].reshape(ne * cap, d))
        e0 += ne
    gate = _sigmoid(_dot(_rms(h, gple_ref[...]).astype(BF16), wgate_ref[...]))
    h = h + gate * _dot(p_ref[...].astype(BF16), wproj_ref[...])
    if final:
        h = _rms(h, gfin_ref[...])
    o_ref[0] = h


def _combine_ple(layer, b0, h, ys, rankt, p, g_ple, w_gate, w_proj, g_final, final):
    b, s, d = h.shape
    cap = ys[0].shape[2]
    n_exp = sum(y.shape[0] for y in ys)
    ple = p.shape[3]
    ts = _row_tile(s, 1024)
    const = lambda i, j: (0, 0)
    return pl.pallas_call(
        functools.partial(_combine_ple_kernel, final, len(ys)),
        grid=(b, s // ts),
        in_specs=[
            pl.BlockSpec((1, ts, d), lambda i, j: (i, j, 0)),
            *[pl.BlockSpec((y.shape[0], 1, cap, d), lambda i, j: (0, i, 0, 0)) for y in ys],
            pl.BlockSpec((1, ts, n_exp), lambda i, j: (i, j, 0)),
            pl.BlockSpec((None, None, ts, ple), lambda i, j: (layer, b0 + i, j, 0)),
            pl.BlockSpec((1, d), const),
            pl.BlockSpec((d, d), const),
            pl.BlockSpec((ple, d), const),
            pl.BlockSpec((1, d), const),
        ],
        out_specs=pl.BlockSpec((1, ts, d), lambda i, j: (i, j, 0)),
        out_shape=jax.ShapeDtypeStruct((b, s, d), F32),
        compiler_params=_cparams("parallel", "arbitrary"),
        name="combine_ple",
    )(h, *ys, rankt, p, g_ple, w_gate, w_proj, g_final)


def _split_router_weight(w):
    d, n_exp = w.shape
    hi = w.astype(BF16)
    r1 = w - hi.astype(F32)
    mid = r1.astype(BF16)
    lo = (r1 - mid.astype(F32)).astype(BF16)
    pad = jnp.zeros((d, LANES - 3 * n_exp), BF16)
    return jnp.concatenate([hi, mid, lo, pad], axis=1)


def _moe_ple(layer, b0, h, hn, lt, p, n_exp, w_gate, w_up, w_down, g_ple, ple_w_gate, ple_w_proj,
             g_final, final):
    b, s, d = h.shape
    cap = EC_CAPACITY_FACTOR * s // n_exp
    rank, gate = _router_select(lt, n_exp, cap)
    split = n_exp - EXPERT_GROUP_SPLIT if 0 < EXPERT_GROUP_SPLIT < n_exp else n_exp
    groups = [(0, split)] + ([(split, n_exp - split)] if split < n_exp else [])
    table = hn.reshape(b * s, d // 2)
    rank2 = rank.reshape(b * n_exp, s)
    gate2 = gate.reshape(b * n_exp, s)
    gathered = [_gather_sc(table, rank2, gate2, b, s, n_exp, e0, ne, cap) for e0, ne in groups]
    ys = []
    for (e0, ne), (xe, gs) in zip(groups, gathered):
        y = _ffn(layer, e0, xe.reshape(ne, b * cap, d // 2), gs.reshape(ne, b * cap, LANES),
                 w_gate, w_up, w_down)
        ys.append(y.reshape(ne, b, cap, d))
    rankt = jnp.swapaxes(rank, 1, 2)
    return _combine_ple(layer, b0, h, ys, rankt, p, g_ple, ple_w_gate, ple_w_proj, g_final, final)


def _rope_tables(s, half):
    inv = ROPE_BASE ** (-jnp.arange(half, dtype=F32) / half)
    ang = jnp.arange(s).astype(F32)[:, None] * inv[None, :]
    return jnp.cos(ang), jnp.sin(ang)


def _forward(x, b0, p, norm_mix, norm_ffn, norm_ple, final_norm, conv_w_in, conv_w, conv_b,
             conv_w_out, ret_w_in, ret_log_decay, ret_w_out, router_w, exp_w_gate, exp_w_up,
             exp_w_down, ple_w_proj, ple_w_gate):
    b, s, d = x.shape
    depth = p.shape[0]
    heads = ret_log_decay.shape[2]
    n_exp = router_w.shape[2]
    hq = d
    hv = (ret_w_in.shape[2] - 2 * hq) // 2
    h = x
    for i in range(depth):
        j = i // 2
        g_mix = norm_mix[i][None, :]
        g_ffn = norm_ffn[i][None, :]
        w_router = _split_router_weight(router_w[i])
        if i % 2 == 0:
            h, hn, lt = _conv_mixer(h, g_mix, conv_w_in[j].astype(BF16), conv_w[j],
                                      conv_b[j][None, :], conv_w_out[j].astype(BF16),
                                      g_ffn, w_router, 3 * n_exp)
        else:
            w_in = ret_w_in[j]
            wq = w_in[:, :hq].astype(BF16)
            wkt = w_in[:, hq:2 * hq].T.astype(BF16)
            wv = w_in[:, 2 * hq:2 * hq + hv].astype(BF16)
            wg = w_in[:, 2 * hq + hv:].astype(BF16)
            cos, sin = _rope_tables(s, hq // heads // 2)
            q, kt, v, sg = _ret_in(h, g_mix, wq, wkt, wv, wg, cos, sin, heads)
            o = _retention(q, kt, v, sg, ret_log_decay[j])
            h, hn, lt = _ret_out(o, ret_w_out[j].astype(BF16), h, g_ffn, w_router, 3 * n_exp)
        h = _moe_ple(i, b0, h, hn, lt, p, n_exp, exp_w_gate, exp_w_up, exp_w_down,
                     norm_ple[i][None, :],
                     ple_w_gate[i].astype(BF16), ple_w_proj[i].astype(BF16),
                     final_norm[None, :], i == depth - 1)
    return h


def kernel(x, p, norm_mix, norm_ffn, norm_ple, final_norm, conv_w_in, conv_w, conv_b, conv_w_out,
           ret_w_in, ret_log_decay, ret_w_out, router_w, exp_w_gate, exp_w_up, exp_w_down,
           ple_w_proj, ple_w_gate):
    return _forward(x, 0, p, norm_mix, norm_ffn, norm_ple, final_norm, conv_w_in, conv_w, conv_b,
                    conv_w_out, ret_w_in, ret_log_decay, ret_w_out, router_w, exp_w_gate, exp_w_up,
                    exp_w_down, ple_w_proj, ple_w_gate)
```

```python
import functools

import jax
import jax.numpy as jnp
from jax import lax
from jax.experimental import pallas as pl
from jax.experimental.pallas import tpu as pltpu
from jax.experimental.pallas import tpu_sc as plsc

F32 = jnp.float32
BF16 = jnp.bfloat16
I32 = jnp.int32

NORM_EPS = 1e-6
GN_EPS = 1e-5
ROPE_BASE = 10000.0
EC_CAPACITY_FACTOR = 2
RET_CHUNK = 256
LANES = 128
SUBLANES = 8
SC_LANES = 16
SC_GATHER_WINDOW = 64
EXPERT_GROUP_SPLIT = 4
CONV_FINISH_SUBTILE = 256
RET_IN_SUBTILE = 256
RET_OUT_SUBTILE = 256
FFN_WEIGHT_CHUNK = 256
VMEM_LIMIT = 56 * 1024 * 1024


def _cparams(*sem):
    return pltpu.CompilerParams(dimension_semantics=sem, vmem_limit_bytes=VMEM_LIMIT)


def _row_tile(n, target):
    t = min(n, target)
    while n % t:
        t //= 2
    return t


def _rms(x, g):
    return x * lax.rsqrt(jnp.mean(x * x, axis=-1, keepdims=True) + NORM_EPS) * g


def _sigmoid(x):
    return 1.0 / (1.0 + jnp.exp(-x))


def _dot(a, b):
    return jnp.dot(a, b, preferred_element_type=F32)


def _pack_bf16_pairs(x):
    w = x.shape[1] // 2
    lo = lax.bitcast_convert_type(x[:, :w].astype(BF16).astype(F32), I32)
    hi = lax.bitcast_convert_type(x[:, w:].astype(BF16).astype(F32), I32)
    return lax.shift_right_logical(lo, jnp.int32(16)) | (hi & jnp.int32(-65536))


def _unpack_bf16_pairs(p):
    lo = lax.bitcast_convert_type(lax.shift_left(p, jnp.int32(16)), F32).astype(BF16)
    hi = lax.bitcast_convert_type(p & jnp.int32(-65536), F32).astype(BF16)
    return jnp.concatenate([lo, hi], axis=1)


def _router_epilogue(h, g_ref, wr_ref, hn_ref, l_ref, rows=slice(None)):
    hn = _rms(h, g_ref[...])
    hi = hn.astype(BF16)
    lo = (hn - hi.astype(F32)).astype(BF16)
    hn_ref[0, rows, :] = _pack_bf16_pairs(hn)
    logits = _dot(hi, wr_ref[...]) + _dot(lo, wr_ref[...])
    l_ref[0, :, rows] = logits.T[:l_ref.shape[1], :]


def _conv_mixer_kernel(xc_ref, xp_ref, g_ref, win_ref, cw_ref, cb_ref, wout_ref, gffn_ref, wr_ref,
                       o_ref, hn_ref, l_ref, u_ref, bg_ref, edge_ref):
    j = pl.program_id(1)
    nj = pl.num_programs(1) - 1
    ts, d = xc_ref.shape[1], xc_ref.shape[2]
    slot = j % 2

    def project():
        hn = _rms(xc_ref[0], g_ref[...]).astype(BF16)
        bg_ref[slot] = _dot(hn, win_ref[:, :d]).astype(BF16)
        u_new = _dot(hn, win_ref[:, d:2 * d]) * _dot(hn, win_ref[:, 2 * d:])
        u_ref[slot] = u_new
        return u_new[0:1, :]

    def finish(next_row):
        u = u_ref[1 - slot]
        prev_row = edge_ref[0:1, :]
        row = lax.broadcasted_iota(I32, u.shape, 0)
        u_prev = jnp.where(row == 0, prev_row, pltpu.roll(u, 1, axis=0))
        u_next = jnp.where(row == ts - 1, next_row, pltpu.roll(u, ts - 1, axis=0))
        cw = cw_ref[...]
        edge_ref[0:1, :] = u[ts - 1:ts, :]
        sub = min(ts, CONV_FINISH_SUBTILE)
        for r in range(0, ts, sub):
            rows = slice(r, r + sub)
            y = (cw[0:1, :] * u_prev[rows] + cw[1:2, :] * u[rows] + cw[2:3, :] * u_next[rows]
                 + cb_ref[...])
            z = (bg_ref[1 - slot, rows, :].astype(F32) * y).astype(BF16)
            h = xp_ref[0, rows, :] + _dot(z, wout_ref[...])
            o_ref[0, rows, :] = h
            _router_epilogue(h, gffn_ref, wr_ref, hn_ref, l_ref, rows)

    @pl.when(j == 0)
    def _():
        edge_ref[...] = jnp.zeros_like(edge_ref)
        project()

    @pl.when(jnp.logical_and(j > 0, j < nj))
    def _():
        finish(project())

    @pl.when(j == nj)
    def _():
        finish(jnp.zeros((1, d), F32))


def _conv_mixer(x, g, w_in, conv_w, conv_b, w_out, g_ffn, w_router, l_rows):
    b, s, d = x.shape
    ts = _row_tile(s, 512)
    nj = s // ts
    cur = lambda i, j: (i, jnp.minimum(j, nj - 1), 0)
    prev = lambda i, j: (i, jnp.maximum(j - 1, 0), 0)
    const = lambda i, j: (0, 0)
    return pl.pallas_call(
        _conv_mixer_kernel,
        grid=(b, nj + 1),
        in_specs=[
            pl.BlockSpec((1, ts, d), cur),
            pl.BlockSpec((1, ts, d), prev),
            pl.BlockSpec((1, d), const),
            pl.BlockSpec((d, 3 * d), const),
            pl.BlockSpec((3, d), const),
            pl.BlockSpec((1, d), const),
            pl.BlockSpec((d, d), const),
            pl.BlockSpec((1, d), const),
            pl.BlockSpec(w_router.shape, const),
        ],
        out_specs=[
            pl.BlockSpec((1, ts, d), prev),
            pl.BlockSpec((1, ts, d // 2), prev),
            pl.BlockSpec((1, l_rows, ts), lambda i, j: (i, 0, jnp.maximum(j - 1, 0))),
        ],
        out_shape=[
            jax.ShapeDtypeStruct((b, s, d), F32),
            jax.ShapeDtypeStruct((b, s, d // 2), I32),
            jax.ShapeDtypeStruct((b, l_rows, s), F32),
        ],
        scratch_shapes=[
            pltpu.VMEM((2, ts, d), F32),
            pltpu.VMEM((2, ts, d), BF16),
            pltpu.VMEM((SUBLANES, d), F32),
        ],
        compiler_params=_cparams("parallel", "arbitrary"),
        name="conv_mixer",
    )(x, x, g, w_in, conv_w, conv_b, w_out, g_ffn, w_router)


def _ret_in_kernel(x_ref, g_ref, wq_ref, wkt_ref, wv_ref, wg_ref,
                   cos_ref, sin_ref, cost_ref, sint_ref,
                   q_ref, kt_ref, v_ref, sg_ref):
    heads, tm, dk = q_ref.shape[1:]
    dv = v_ref.shape[3]
    half = dk // 2
    scale = dk ** -0.5
    sub = min(tm, RET_IN_SUBTILE)
    for r in range(0, tm, sub):
        rows = slice(r, r + sub)
        hn = _rms(x_ref[0, rows, :], g_ref[...]).astype(BF16)
        q = _dot(hn, wq_ref[...])
        cos = cos_ref[rows, :]
        sin = sin_ref[rows, :]
        for h in range(heads):
            x1 = q[:, h * dk:h * dk + half]
            x2 = q[:, h * dk + half:(h + 1) * dk]
            q_ref[0, h, rows, :half] = (x1 * cos - x2 * sin).astype(BF16)
            q_ref[0, h, rows, half:] = (x1 * sin + x2 * cos).astype(BF16)
        kt = lax.dot_general(wkt_ref[...], hn, (((1,), (1,)), ((), ())),
                             preferred_element_type=F32)
        cost = cost_ref[:, rows]
        sint = sint_ref[:, rows]
        for h in range(heads):
            x1 = kt[h * dk:h * dk + half, :]
            x2 = kt[h * dk + half:(h + 1) * dk, :]
            kt_ref[0, h, :half, rows] = ((x1 * cost - x2 * sint) * scale).astype(BF16)
            kt_ref[0, h, half:, rows] = ((x1 * sint + x2 * cost) * scale).astype(BF16)
        v = _dot(hn, wv_ref[...])
        gate = _dot(hn, wg_ref[...])
        sg = gate * _sigmoid(gate)
        for h in range(heads):
            v_ref[0, h, rows, :] = v[:, h * dv:(h + 1) * dv].astype(BF16)
            sg_ref[0, h, rows, :] = sg[:, h * dv:(h + 1) * dv].astype(BF16)


def _ret_in(x, g, wq, wkt, wv, wg, cos, sin, heads):
    b, s, d = x.shape
    hq = wq.shape[1]
    hv = wv.shape[1]
    dk = hq // heads
    dv = hv // heads
    half = cos.shape[1]
    tm = _row_tile(s, 1024)
    cost = cos.T
    sint = sin.T
    const = lambda i, j: (0, 0)
    return pl.pallas_call(
        _ret_in_kernel,
        grid=(b, s // tm),
        in_specs=[
            pl.BlockSpec((1, tm, d), lambda i, j: (i, j, 0)),
            pl.BlockSpec((1, d), const),
            pl.BlockSpec((d, hq), const, pipeline_mode=pl.Buffered(1)),
            pl.BlockSpec((hq, d), const, pipeline_mode=pl.Buffered(1)),
            pl.BlockSpec((d, hv), const, pipeline_mode=pl.Buffered(1)),
            pl.BlockSpec((d, hv), const, pipeline_mode=pl.Buffered(1)),
            pl.BlockSpec((tm, half), lambda i, j: (j, 0)),
            pl.BlockSpec((tm, half), lambda i, j: (j, 0)),
            pl.BlockSpec((half, tm), lambda i, j: (0, j)),
            pl.BlockSpec((half, tm), lambda i, j: (0, j)),
        ],
        out_specs=[
            pl.BlockSpec((1, heads, tm, dk), lambda i, j: (i, 0, j, 0)),
            pl.BlockSpec((1, heads, dk, tm), lambda i, j: (i, 0, 0, j)),
            pl.BlockSpec((1, heads, tm, dv), lambda i, j: (i, 0, j, 0)),
            pl.BlockSpec((1, heads, tm, dv), lambda i, j: (i, 0, j, 0)),
        ],
        out_shape=[
            jax.ShapeDtypeStruct((b, heads, s, dk), BF16),
            jax.ShapeDtypeStruct((b, heads, dk, s), BF16),
            jax.ShapeDtypeStruct((b, heads, s, dv), BF16),
            jax.ShapeDtypeStruct((b, heads, s, dv), BF16),
        ],
        compiler_params=_cparams("parallel", "parallel"),
        name="ret_in",
    )(x, g, wq, wkt, wv, wg, cos, sin, cost, sint)


def _retention_kernel(chunk, ld_ref, q_ref, kt_ref, v_ref, sg_ref, o_ref, acc_ref, sf_ref, sb_ref):
    h = pl.program_id(1)
    s = q_ref.shape[2]
    L = chunk
    nc = s // L
    lgf = ld_ref[0, h]
    lgb = ld_ref[1, h]
    ii = lax.broadcasted_iota(I32, (L, L), 0)
    jj = lax.broadcasted_iota(I32, (L, L), 1)
    dist = (ii - jj).astype(F32)
    causal = ii >= jj
    decay = jnp.where(causal,
                      jnp.exp(lgf * jnp.where(causal, dist, 0.0)),
                      jnp.exp(lgb * jnp.where(causal, 0.0, -dist)))
    col = lax.broadcasted_iota(I32, (L, 1), 0).astype(F32)
    row = lax.broadcasted_iota(I32, (1, L), 1).astype(F32)
    q_dec_f = jnp.exp(lgf * (col + 1.0)).astype(BF16)
    q_dec_b = jnp.exp(lgb * (L - col)).astype(BF16)
    k_dec_f = jnp.exp(lgf * (L - 1.0 - row)).astype(BF16)
    k_dec_b = jnp.exp(lgb * row).astype(BF16)
    one = jnp.ones((1, 1), F32)
    c_dec_f = jnp.exp(lgf * L * one)
    c_dec_b = jnp.exp(lgb * L * one)

    def finish(c, out):
        sl = slice(c * L, (c + 1) * L)
        mu = jnp.mean(out, axis=-1, keepdims=True)
        cen = out - mu
        var = jnp.mean(cen * cen, axis=-1, keepdims=True)
        o_ref[0, 0, sl, :] = (sg_ref[0, 0, sl, :].astype(F32) * (cen * lax.rsqrt(var + GN_EPS))
                              ).astype(BF16)

    stored = {}

    def contribute(c, part):
        sl = slice(c * L, (c + 1) * L)
        if c not in stored:
            stored[c] = part is not None
            if part is not None:
                acc_ref[sl, :] = part
        elif not stored[c]:
            finish(c, part)
        else:
            finish(c, acc_ref[sl, :] if part is None else acc_ref[sl, :] + part)

    for k in range(nc):
        cf, cb = k, nc - 1 - k
        slf = slice(cf * L, (cf + 1) * L)
        slb = slice(cb * L, (cb + 1) * L)
        qc, ktc, vc = q_ref[0, 0, slf, :], kt_ref[0, 0, :, slf], v_ref[0, 0, slf, :]
        scores = (_dot(qc, ktc) * decay).astype(BF16)
        part_f = _dot(scores, vc)
        if cf > 0:
            part_f += _dot(qc * q_dec_f, sf_ref[...].astype(BF16))
        if cf < nc - 1:
            upd = _dot(ktc * k_dec_f, vc)
            sf_ref[...] = upd if cf == 0 else sf_ref[...] * c_dec_f + upd
        part_b = None
        if cb < nc - 1:
            part_b = _dot(q_ref[0, 0, slb, :] * q_dec_b, sb_ref[...].astype(BF16))
        if cb > 0:
            upd = _dot(kt_ref[0, 0, :, slb] * k_dec_b, v_ref[0, 0, slb, :])
            sb_ref[...] = upd if cb == nc - 1 else sb_ref[...] * c_dec_b + upd
        contribute(cf, part_f)
        contribute(cb, part_b)


def _retention(q, kt, v, sg, log_decay):
    b, heads, s, dk = q.shape
    dv = v.shape[3]
    chunk = min(RET_CHUNK, s)
    return pl.pallas_call(
        functools.partial(_retention_kernel, chunk),
        grid=(b, heads),
        in_specs=[
            pl.BlockSpec(memory_space=pltpu.SMEM),
            pl.BlockSpec((1, 1, s, dk), lambda i, h: (i, h, 0, 0)),
            pl.BlockSpec((1, 1, dk, s), lambda i, h: (i, h, 0, 0)),
            pl.BlockSpec((1, 1, s, dv), lambda i, h: (i, h, 0, 0)),
            pl.BlockSpec((1, 1, s, dv), lambda i, h: (i, h, 0, 0)),
        ],
        out_specs=pl.BlockSpec((1, 1, s, dv), lambda i, h: (i, h, 0, 0)),
        out_shape=jax.ShapeDtypeStruct((b, heads, s, dv), BF16),
        scratch_shapes=[
            pltpu.VMEM((s, dv), F32),
            pltpu.VMEM((dk, dv), F32),
            pltpu.VMEM((dk, dv), F32),
        ],
        compiler_params=_cparams("parallel", "parallel"),
        name="retention",
    )(log_decay, q, kt, v, sg)


def _ret_out_kernel(o_ref, w_ref, x_ref, gffn_ref, wr_ref, h_ref, hn_ref, l_ref):
    heads, ts = o_ref.shape[1], o_ref.shape[2]
    sub = min(ts, RET_OUT_SUBTILE)
    for r in range(0, ts, sub):
        rows = slice(r, r + sub)
        h = x_ref[0, rows, :]
        for hd in range(heads):
            h = h + _dot(o_ref[0, hd, rows, :], w_ref[hd])
        h_ref[0, rows, :] = h
        _router_epilogue(h, gffn_ref, wr_ref, hn_ref, l_ref, rows)


def _ret_out(o, w_out, x, g_ffn, w_router, l_rows):
    b, heads, s, dv = o.shape
    d = x.shape[2]
    ts = _row_tile(s, 1024)
    tile = lambda i, j: (i, j, 0)
    const = lambda i, j: (0, 0)
    return pl.pallas_call(
        _ret_out_kernel,
        grid=(b, s // ts),
        in_specs=[
            pl.BlockSpec((1, heads, ts, dv), lambda i, j: (i, 0, j, 0)),
            pl.BlockSpec((heads, dv, d), lambda i, j: (0, 0, 0)),
            pl.BlockSpec((1, ts, d), tile),
            pl.BlockSpec((1, d), const),
            pl.BlockSpec(w_router.shape, const),
        ],
        out_specs=[
            pl.BlockSpec((1, ts, d), tile),
            pl.BlockSpec((1, ts, d // 2), tile),
            pl.BlockSpec((1, l_rows, ts), lambda i, j: (i, 0, j)),
        ],
        out_shape=[
            jax.ShapeDtypeStruct((b, s, d), F32),
            jax.ShapeDtypeStruct((b, s, d // 2), I32),
            jax.ShapeDtypeStruct((b, l_rows, s), F32),
        ],
        compiler_params=_cparams("parallel", "parallel"),
        name="ret_out",
    )(o, w_out.reshape(heads, dv, d), x, g_ffn, w_router)


def _exclusive_prefix(flags):
    rows, s = flags.shape
    upper = (lax.broadcasted_iota(I32, (LANES, LANES), 0)
             < lax.broadcasted_iota(I32, (LANES, LANES), 1))
    upper = jnp.where(upper, 1.0, 0.0).astype(BF16)
    carry = jnp.zeros((rows, 1), F32)
    pieces = []
    for k in range(s // LANES):
        blk = flags[:, k * LANES:(k + 1) * LANES]
        pieces.append(_dot(blk.astype(BF16), upper) + carry)
        carry = carry + jnp.sum(blk, axis=1, keepdims=True)
    return jnp.concatenate(pieces, axis=1)


def _router_select_kernel(n_exp, cap, l_ref, rank_ref, gate_ref):
    nb = l_ref.shape[0]
    affs = []
    for i in range(nb):
        l = l_ref[i]
        logits = l[0:n_exp] + l[n_exp:2 * n_exp] + l[2 * n_exp:3 * n_exp]
        e = jnp.exp(logits - jnp.max(logits, axis=0, keepdims=True))
        affs.append(e / jnp.sum(e, axis=0, keepdims=True))
    aff = jnp.concatenate(affs, axis=0)
    keys = pltpu.bitcast(aff, I32)

    def refine(i, thr):
        cand = thr | jnp.left_shift(jnp.int32(1), 30 - i)
        cnt = jnp.sum(jnp.where(keys >= cand, 1.0, 0.0), axis=1, keepdims=True)
        return jnp.where(cnt >= cap, cand, thr)

    thr = lax.fori_loop(0, 31, refine, jnp.zeros((nb * n_exp, 1), I32))
    above = keys > thr
    tied = keys == thr
    n_above = jnp.sum(jnp.where(above, 1.0, 0.0), axis=1, keepdims=True)
    tied_rank = _exclusive_prefix(jnp.where(tied, 1.0, 0.0))
    keep = above | (tied & (tied_rank < cap - n_above))
    rank = _exclusive_prefix(jnp.where(keep, 1.0, 0.0))
    rank = jnp.where(keep, rank.astype(I32), -1)
    gate = jnp.where(keep, aff, 0.0)
    for i in range(nb):
        rank_ref[i] = rank[i * n_exp:(i + 1) * n_exp]
        gate_ref[i] = gate[i * n_exp:(i + 1) * n_exp]


def _router_select(lt, n_exp, cap):
    b, rows, s = lt.shape
    nb = _row_tile(b, 8)
    blk = lambda i: (i, 0, 0)
    return pl.pallas_call(
        functools.partial(_router_select_kernel, n_exp, cap),
        grid=(b // nb,),
        in_specs=[pl.BlockSpec((nb, rows, s), blk)],
        out_specs=[
            pl.BlockSpec((nb, n_exp, s), blk),
            pl.BlockSpec((nb, n_exp, s), blk),
        ],
        out_shape=[
            jax.ShapeDtypeStruct((b, n_exp, s), I32),
            jax.ShapeDtypeStruct((b, n_exp, s), F32),
        ],
        compiler_params=_cparams("parallel"),
        name="router_select",
    )(lt)


def _gather_sc_kernel(n_seq, seq_len, n_exp, e0, ne, cap, win, n_cores, pairs_per_worker,
                      table_hbm, rank_hbm, gate_hbm, zeros_hbm, xe_hbm, gs_hbm,
                      rank_v, gate_v, tok_v, rows_v, gs_v, gsem, osem):
    lanes = SC_LANES
    wid = lax.axis_index("s") * n_cores + lax.axis_index("c")
    lane = lax.broadcasted_iota(I32, (lanes,), 0)
    lane0 = jnp.zeros((lanes,), I32)
    pltpu.sync_copy(zeros_hbm, gs_v)
    n_win = cap // win

    @pl.loop(0, pairs_per_worker)
    def _(i):
        pair = wid * pairs_per_worker + i
        b = pair // ne
        e = pair % ne
        row = b * n_exp + e0 + e
        pltpu.sync_copy(rank_hbm.at[row], rank_v)
        pltpu.sync_copy(gate_hbm.at[row], gate_v)

        @pl.loop(0, seq_len // lanes)
        def _(j):
            r = rank_v[pl.ds(j * lanes, lanes)]
            keep = r >= 0
            slot = jnp.where(keep, r, 0)
            plsc.store_scatter(tok_v, [slot // win, slot % win], lane + (j * lanes + b * seq_len),
                               mask=keep)
            plsc.store_scatter(gs_v, [slot, lane0], gate_v[pl.ds(j * lanes, lanes)], mask=keep)

        out_base = (e * n_seq + b) * cap

        def fetch(w):
            return pltpu.make_async_copy(table_hbm.at[tok_v.at[w]], rows_v.at[w % 2], gsem.at[w % 2])

        def put(w):
            return pltpu.make_async_copy(rows_v.at[w % 2], xe_hbm.at[pl.ds(out_base + w * win, win)],
                                         osem.at[w % 2])

        fetch(0).start()
        for w in range(n_win):
            if w + 1 < n_win:
                if w >= 1:
                    put(w - 1).wait()
                fetch(w + 1).start()
            fetch(w).wait()
            put(w).start()
        if n_win >= 2:
            put(n_win - 2).wait()
        put(n_win - 1).wait()
        pltpu.sync_copy(gs_v, gs_hbm.at[pl.ds(out_base, cap)])


def _gather_sc(table, rank, gate, n_seq, seq_len, n_exp, e0, ne, cap):
    width = table.shape[1]
    info = plsc.get_sparse_core_info()
    n_workers = info.num_cores * info.num_subcores
    pairs = n_seq * ne
    assert info.num_lanes == SC_LANES and pairs % n_workers == 0 and seq_len % SC_LANES == 0
    win = min(cap, SC_GATHER_WINDOW)
    assert cap % win == 0
    mesh = plsc.VectorSubcoreMesh(core_axis_name="c", subcore_axis_name="s")
    call = pl.kernel(
        functools.partial(_gather_sc_kernel, n_seq, seq_len, n_exp, e0, ne, cap, win,
                          info.num_cores, pairs // n_workers),
        mesh=mesh,
        out_type=[jax.ShapeDtypeStruct((ne * n_seq * cap, width), I32),
                  jax.ShapeDtypeStruct((ne * n_seq * cap, LANES), F32)],
        scratch_types=[
            pltpu.VMEM((seq_len,), I32),
            pltpu.VMEM((seq_len,), F32),
            pltpu.VMEM((cap // win, win), I32),
            pltpu.VMEM((2, win, width), I32),
            pltpu.VMEM((cap, LANES), F32),
            pltpu.SemaphoreType.DMA((2,)),
            pltpu.SemaphoreType.DMA((2,)),
        ],
        compiler_params=pltpu.CompilerParams(needs_layout_passes=False),
        name="moe_gather_sc",
    )
    return call(table, rank, gate, jnp.zeros((cap, LANES), F32))


def _ffn_kernel(layer, e0, xe_ref, gs_ref, wg_hbm, wu_hbm, wd_hbm, y_ref,
                wg_s, wu_s, wd_s, stg_g, stg_u, stg_d, sem):
    e = pl.program_id(0)
    m = pl.program_id(1)
    n_exp = pl.num_programs(0)
    nm = pl.num_programs(1)
    n_chunks, d, fc = wg_s.shape[1:]
    cur = e % 2
    par = m % 2

    def chunk_copies(exp, k, buf):
        cols = pl.ds(pl.multiple_of(k * fc, fc), fc)
        return (pltpu.make_async_copy(wg_hbm.at[layer, e0 + exp, :, cols], stg_g.at[buf], sem.at[buf, 0]),
                pltpu.make_async_copy(wu_hbm.at[layer, e0 + exp, :, cols], stg_u.at[buf], sem.at[buf, 1]),
                pltpu.make_async_copy(wd_hbm.at[layer, e0 + exp, cols, :], stg_d.at[buf], sem.at[buf, 2]))

    def convert(buf, slot, k):
        wg_s[slot, k] = stg_g[buf].astype(BF16)
        wu_s[slot, k] = stg_u[buf].astype(BF16)
        wd_s[slot, pl.ds(pl.multiple_of(k * fc, fc), fc), :] = stg_d[buf].astype(BF16)

    first = jnp.logical_and(e == 0, m == 0)

    @pl.when(first)
    def _():
        for c in chunk_copies(0, 0, 0):
            c.start()
        for k in range(n_chunks):
            if k + 1 < n_chunks:
                for c in chunk_copies(0, k + 1, (k + 1) % 2):
                    c.start()
            for c in chunk_copies(0, k, k % 2):
                c.wait()
            if k < n_chunks - 1:
                convert(k % 2, 0, k)

    in_flight = jnp.where(m == 0, e > 0, e < n_exp - 1)

    @pl.when(in_flight)
    def _():
        for c in chunk_copies(0, 0, 1 - par):
            c.wait()

    @pl.when(e < n_exp - 1)
    def _():
        for c in chunk_copies(e + 1, m, par):
            c.start()

    convert(1 - par, jnp.where(m == 0, cur, 1 - cur), jnp.where(m == 0, nm - 1, m - 1))

    x = _unpack_bf16_pairs(xe_ref[0])
    acc = jnp.zeros((x.shape[0], d), F32)
    for k in range(0, n_chunks, 2):
        acts = []
        for kk in (k, k + 1):
            a = _dot(x, wg_s[cur, kk])
            acts.append(a * _sigmoid(a) * _dot(x, wu_s[cur, kk]))
        act = jnp.concatenate(acts, axis=1).astype(BF16)
        acc += _dot(act, wd_s[cur, k * fc:(k + 2) * fc, :])
    y_ref[0] = (acc * gs_ref[0][:, 0:1]).astype(BF16)


def _ffn(layer, e0, xe, gs, wg, wu, wd):
    n_exp, rows, half_d = xe.shape
    d = 2 * half_d
    ff = wg.shape[3]
    fc = min(ff // 2, FFN_WEIGHT_CHUNK)
    nm = ff // fc
    assert ff % fc == 0 and nm % 2 == 0 and rows % nm == 0
    tm = rows // nm
    assert tm % (2 * SUBLANES) == 0
    return pl.pallas_call(
        functools.partial(_ffn_kernel, layer, e0),
        grid=(n_exp, nm),
        in_specs=[
            pl.BlockSpec((1, tm, half_d), lambda e, m: (e, m, 0)),
            pl.BlockSpec((1, tm, LANES), lambda e, m: (e, m, 0)),
            pl.BlockSpec(memory_space=pl.ANY),
            pl.BlockSpec(memory_space=pl.ANY),
            pl.BlockSpec(memory_space=pl.ANY),
        ],
        out_specs=pl.BlockSpec((1, tm, d), lambda e, m: (e, m, 0)),
        out_shape=jax.ShapeDtypeStruct((n_exp, rows, d), BF16),
        scratch_shapes=[
            pltpu.VMEM((2, nm, d, fc), BF16),
            pltpu.VMEM((2, nm, d, fc), BF16),
            pltpu.VMEM((2, ff, d), BF16),
            pltpu.VMEM((2, d, fc), F32),
            pltpu.VMEM((2, d, fc), F32),
            pltpu.VMEM((2, fc, d), F32),
            pltpu.SemaphoreType.DMA((2, 3)),
        ],
        compiler_params=_cparams("arbitrary", "arbitrary"),
        name="moe_ffn",
    )(xe, gs, wg, wu, wd)


def _combine_ple_kernel(final, n_groups, h_ref, *refs):
    y_refs = refs[:n_groups]
    rankt_ref, p_ref, gple_ref, wgate_ref, wproj_ref, gfin_ref, o_ref = refs[n_groups:]
    _, _, cap, d = y_refs[0].shape
    ts = h_ref.shape[1]
    rt = rankt_ref[0]
    slot = lax.broadcasted_iota(I32, (ts, cap), 1)
    h = h_ref[0]
    e0 = 0
    for y_ref in y_refs:
        ne = y_ref.shape[0]
        onehot = jnp.concatenate(
            [jnp.where(rt[:, e:e + 1] == slot, 1.0, 0.0).astype(BF16) for e in range(e0, e0 + ne)],
            axis=1)
        h = h + _dot(onehot, y_ref[:, 0].reshape(ne * cap, d))
        e0 += ne
    gate = _sigmoid(_dot(_rms(h, gple_ref[...]).astype(BF16), wgate_ref[...]))
    h = h + gate * _dot(p_ref[...].astype(BF16), wproj_ref[...])
    if final:
        h = _rms(h, gfin_ref[...])
    o_ref[0] = h


def _combine_ple(layer, h, ys, rankt, p, g_ple, w_gate, w_proj, g_final, final):
    b, s, d = h.shape
    cap = ys[0].shape[2]
    n_exp = sum(y.shape[0] for y in ys)
    ple = p.shape[3]
    ts = _row_tile(s, 1024)
    const = lambda i, j: (0, 0)
    return pl.pallas_call(
        functools.partial(_combine_ple_kernel, final, len(ys)),
        grid=(b, s // ts),
        in_specs=[
            pl.BlockSpec((1, ts, d), lambda i, j: (i, j, 0)),
            *[pl.BlockSpec((y.shape[0], 1, cap, d), lambda i, j: (0, i, 0, 0)) for y in ys],
            pl.BlockSpec((1, ts, n_exp), lambda i, j: (i, j, 0)),
            pl.BlockSpec((None, None, ts, ple), lambda i, j: (layer, i, j, 0)),
            pl.BlockSpec((1, d), const),
            pl.BlockSpec((d, d), const),
            pl.BlockSpec((ple, d), const),
            pl.BlockSpec((1, d), const),
        ],
        out_specs=pl.BlockSpec((1, ts, d), lambda i, j: (i, j, 0)),
        out_shape=jax.ShapeDtypeStruct((b, s, d), F32),
        compiler_params=_cparams("parallel", "arbitrary"),
        name="combine_ple",
    )(h, *ys, rankt, p, g_ple, w_gate, w_proj, g_final)


def _split_router_weight(w):
    d, n_exp = w.shape
    hi = w.astype(BF16)
    r1 = w - hi.astype(F32)
    mid = r1.astype(BF16)
    lo = (r1 - mid.astype(F32)).astype(BF16)
    pad = jnp.zeros((d, LANES - 3 * n_exp), BF16)
    return jnp.concatenate([hi, mid, lo, pad], axis=1)


def _moe_ple(layer, h, hn, lt, p, n_exp, w_gate, w_up, w_down, g_ple, ple_w_gate, ple_w_proj,
             g_final, final):
    b, s, d = h.shape
    cap = EC_CAPACITY_FACTOR * s // n_exp
    rank, gate = _router_select(lt, n_exp, cap)
    split = n_exp - EXPERT_GROUP_SPLIT if 0 < EXPERT_GROUP_SPLIT < n_exp else n_exp
    groups = [(0, split)] + ([(split, n_exp - split)] if split < n_exp else [])
    table = hn.reshape(b * s, d // 2)
    rank2 = rank.reshape(b * n_exp, s)
    gate2 = gate.reshape(b * n_exp, s)
    gathered = [_gather_sc(table, rank2, gate2, b, s, n_exp, e0, ne, cap) for e0, ne in groups]
    ys = []
    for (e0, ne), (xe, gs) in zip(groups, gathered):
        y = _ffn(layer, e0, xe.reshape(ne, b * cap, d // 2), gs.reshape(ne, b * cap, LANES),
                 w_gate, w_up, w_down)
        ys.append(y.reshape(ne, b, cap, d))
    rankt = jnp.swapaxes(rank, 1, 2)
    return _combine_ple(layer, h, ys, rankt, p, g_ple, ple_w_gate, ple_w_proj, g_final, final)


def _rope_tables(s, half):
    inv = ROPE_BASE ** (-jnp.arange(half, dtype=F32) / half)
    ang = jnp.arange(s).astype(F32)[:, None] * inv[None, :]
    return jnp.cos(ang), jnp.sin(ang)


def kernel(x, p, norm_mix, norm_ffn, norm_ple, final_norm, conv_w_in, conv_w, conv_b, conv_w_out,
           ret_w_in, ret_log_decay, ret_w_out, router_w, exp_w_gate, exp_w_up, exp_w_down,
           ple_w_proj, ple_w_gate):
    b, s, d = x.shape
    depth = p.shape[0]
    heads = ret_log_decay.shape[2]
    n_exp = router_w.shape[2]
    hq = d
    hv = (ret_w_in.shape[2] - 2 * hq) // 2
    h = x
    for i in range(depth):
        j = i // 2
        g_mix = norm_mix[i][None, :]
        g_ffn = norm_ffn[i][None, :]
        w_router = _split_router_weight(router_w[i])
        if i % 2 == 0:
            h, hn, lt = _conv_mixer(h, g_mix, conv_w_in[j].astype(BF16), conv_w[j],
                                      conv_b[j][None, :], conv_w_out[j].astype(BF16),
                                      g_ffn, w_router, 3 * n_exp)
        else:
            w_in = ret_w_in[j]
            wq = w_in[:, :hq].astype(BF16)
            wkt = w_in[:, hq:2 * hq].T.astype(BF16)
            wv = w_in[:, 2 * hq:2 * hq + hv].astype(BF16)
            wg = w_in[:, 2 * hq + hv:].astype(BF16)
            cos, sin = _rope_tables(s, hq // heads // 2)
            q, kt, v, sg = _ret_in(h, g_mix, wq, wkt, wv, wg, cos, sin, heads)
            o = _retention(q, kt, v, sg, ret_log_decay[j])
            h, hn, lt = _ret_out(o, ret_w_out[j].astype(BF16), h, g_ffn, w_router, 3 * n_exp)
        h = _moe_ple(i, h, hn, lt, p, n_exp, exp_w_gate, exp_w_up, exp_w_down,
                     norm_ple[i][None, :],
                     ple_w_gate[i].astype(BF16), ple_w_proj[i].astype(BF16),
                     final_norm[None, :], i == depth - 1)
    return h
```

```python
import functools

import jax
import jax.numpy as jnp
from jax import lax
from jax.experimental import pallas as pl
from jax.experimental.pallas import tpu as pltpu
from jax.experimental.pallas import tpu_sc as plsc

F32 = jnp.float32
BF16 = jnp.bfloat16
I32 = jnp.int32

NORM_EPS = 1e-6
GN_EPS = 1e-5
ROPE_BASE = 10000.0
EC_CAPACITY_FACTOR = 2
RET_CHUNK = 256
LANES = 128
SUBLANES = 8
SC_LANES = 16
SC_GATHER_WINDOW = 64
EXPERT_GROUP_SPLIT = 2
CONV_FINISH_SUBTILE = 256
RET_IN_SUBTILE = 256
RET_OUT_SUBTILE = 256
FFN_WEIGHT_CHUNK = 256
VMEM_LIMIT = 56 * 1024 * 1024


def _cparams(*sem):
    return pltpu.CompilerParams(dimension_semantics=sem, vmem_limit_bytes=VMEM_LIMIT)


def _row_tile(n, target):
    t = min(n, target)
    while n % t:
        t //= 2
    return t


def _rms(x, g):
    return x * lax.rsqrt(jnp.mean(x * x, axis=-1, keepdims=True) + NORM_EPS) * g


def _sigmoid(x):
    return 1.0 / (1.0 + jnp.exp(-x))


def _dot(a, b):
    return jnp.dot(a, b, preferred_element_type=F32)


def _pack_bf16_pairs(x):
    w = x.shape[1] // 2
    lo = lax.bitcast_convert_type(x[:, :w].astype(BF16).astype(F32), I32)
    hi = lax.bitcast_convert_type(x[:, w:].astype(BF16).astype(F32), I32)
    return lax.shift_right_logical(lo, jnp.int32(16)) | (hi & jnp.int32(-65536))


def _unpack_bf16_pairs(p):
    lo = lax.bitcast_convert_type(lax.shift_left(p, jnp.int32(16)), F32).astype(BF16)
    hi = lax.bitcast_convert_type(p & jnp.int32(-65536), F32).astype(BF16)
    return jnp.concatenate([lo, hi], axis=1)


def _router_epilogue(h, g_ref, wr_ref, hn_ref, l_ref, rows=slice(None)):
    hn = _rms(h, g_ref[...])
    hi = hn.astype(BF16)
    lo = (hn - hi.astype(F32)).astype(BF16)
    hn_ref[0, rows, :] = _pack_bf16_pairs(hn)
    logits = _dot(hi, wr_ref[...]) + _dot(lo, wr_ref[...])
    l_ref[0, :, rows] = logits.T[:l_ref.shape[1], :]


def _conv_mixer_kernel(xc_ref, xp_ref, g_ref, win_ref, cw_ref, cb_ref, wout_ref, gffn_ref, wr_ref,
                       o_ref, hn_ref, l_ref, u_ref, bg_ref, edge_ref):
    j = pl.program_id(1)
    nj = pl.num_programs(1) - 1
    ts, d = xc_ref.shape[1], xc_ref.shape[2]
    slot = j % 2

    def project():
        hn = _rms(xc_ref[0], g_ref[...]).astype(BF16)
        bg_ref[slot] = _dot(hn, win_ref[:, :d]).astype(BF16)
        u_new = _dot(hn, win_ref[:, d:2 * d]) * _dot(hn, win_ref[:, 2 * d:])
        u_ref[slot] = u_new
        return u_new[0:1, :]

    def finish(next_row):
        u = u_ref[1 - slot]
        prev_row = edge_ref[0:1, :]
        row = lax.broadcasted_iota(I32, u.shape, 0)
        u_prev = jnp.where(row == 0, prev_row, pltpu.roll(u, 1, axis=0))
        u_next = jnp.where(row == ts - 1, next_row, pltpu.roll(u, ts - 1, axis=0))
        cw = cw_ref[...]
        edge_ref[0:1, :] = u[ts - 1:ts, :]
        sub = min(ts, CONV_FINISH_SUBTILE)
        for r in range(0, ts, sub):
            rows = slice(r, r + sub)
            y = (cw[0:1, :] * u_prev[rows] + cw[1:2, :] * u[rows] + cw[2:3, :] * u_next[rows]
                 + cb_ref[...])
            z = (bg_ref[1 - slot, rows, :].astype(F32) * y).astype(BF16)
            h = xp_ref[0, rows, :] + _dot(z, wout_ref[...])
            o_ref[0, rows, :] = h
            _router_epilogue(h, gffn_ref, wr_ref, hn_ref, l_ref, rows)

    @pl.when(j == 0)
    def _():
        edge_ref[...] = jnp.zeros_like(edge_ref)
        project()

    @pl.when(jnp.logical_and(j > 0, j < nj))
    def _():
        finish(project())

    @pl.when(j == nj)
    def _():
        finish(jnp.zeros((1, d), F32))


def _conv_mixer(x, g, w_in, conv_w, conv_b, w_out, g_ffn, w_router, l_rows):
    b, s, d = x.shape
    ts = _row_tile(s, 512)
    nj = s // ts
    cur = lambda i, j: (i, jnp.minimum(j, nj - 1), 0)
    prev = lambda i, j: (i, jnp.maximum(j - 1, 0), 0)
    const = lambda i, j: (0, 0)
    return pl.pallas_call(
        _conv_mixer_kernel,
        grid=(b, nj + 1),
        in_specs=[
            pl.BlockSpec((1, ts, d), cur),
            pl.BlockSpec((1, ts, d), prev),
            pl.BlockSpec((1, d), const),
            pl.BlockSpec((d, 3 * d), const),
            pl.BlockSpec((3, d), const),
            pl.BlockSpec((1, d), const),
            pl.BlockSpec((d, d), const),
            pl.BlockSpec((1, d), const),
            pl.BlockSpec(w_router.shape, const),
        ],
        out_specs=[
            pl.BlockSpec((1, ts, d), prev),
            pl.BlockSpec((1, ts, d // 2), prev),
            pl.BlockSpec((1, l_rows, ts), lambda i, j: (i, 0, jnp.maximum(j - 1, 0))),
        ],
        out_shape=[
            jax.ShapeDtypeStruct((b, s, d), F32),
            jax.ShapeDtypeStruct((b, s, d // 2), I32),
            jax.ShapeDtypeStruct((b, l_rows, s), F32),
        ],
        scratch_shapes=[
            pltpu.VMEM((2, ts, d), F32),
            pltpu.VMEM((2, ts, d), BF16),
            pltpu.VMEM((SUBLANES, d), F32),
        ],
        compiler_params=_cparams("parallel", "arbitrary"),
        name="conv_mixer",
    )(x, x, g, w_in, conv_w, conv_b, w_out, g_ffn, w_router)


def _ret_in_kernel(x_ref, g_ref, wq_ref, wkt_ref, wv_ref, wg_ref,
                   cos_ref, sin_ref, cost_ref, sint_ref,
                   q_ref, kt_ref, v_ref, sg_ref):
    heads, tm, dk = q_ref.shape[1:]
    dv = v_ref.shape[3]
    half = dk // 2
    scale = dk ** -0.5
    sub = min(tm, RET_IN_SUBTILE)
    for r in range(0, tm, sub):
        rows = slice(r, r + sub)
        hn = _rms(x_ref[0, rows, :], g_ref[...]).astype(BF16)
        q = _dot(hn, wq_ref[...])
        cos = cos_ref[rows, :]
        sin = sin_ref[rows, :]
        for h in range(heads):
            x1 = q[:, h * dk:h * dk + half]
            x2 = q[:, h * dk + half:(h + 1) * dk]
            q_ref[0, h, rows, :half] = (x1 * cos - x2 * sin).astype(BF16)
            q_ref[0, h, rows, half:] = (x1 * sin + x2 * cos).astype(BF16)
        kt = lax.dot_general(wkt_ref[...], hn, (((1,), (1,)), ((), ())),
                             preferred_element_type=F32)
        cost = cost_ref[:, rows]
        sint = sint_ref[:, rows]
        for h in range(heads):
            x1 = kt[h * dk:h * dk + half, :]
            x2 = kt[h * dk + half:(h + 1) * dk, :]
            kt_ref[0, h, :half, rows] = ((x1 * cost - x2 * sint) * scale).astype(BF16)
            kt_ref[0, h, half:, rows] = ((x1 * sint + x2 * cost) * scale).astype(BF16)
        v = _dot(hn, wv_ref[...])
        gate = _dot(hn, wg_ref[...])
        sg = gate * _sigmoid(gate)
        for h in range(heads):
            v_ref[0, h, rows, :] = v[:, h * dv:(h + 1) * dv].astype(BF16)
            sg_ref[0, h, rows, :] = sg[:, h * dv:(h + 1) * dv].astype(BF16)


def _ret_in(x, g, wq, wkt, wv, wg, cos, sin, heads):
    b, s, d = x.shape
    hq = wq.shape[1]
    hv = wv.shape[1]
    dk = hq // heads
    dv = hv // heads
    half = cos.shape[1]
    tm = _row_tile(s, 1024)
    cost = cos.T
    sint = sin.T
    const = lambda i, j: (0, 0)
    return pl.pallas_call(
        _ret_in_kernel,
        grid=(b, s // tm),
        in_specs=[
            pl.BlockSpec((1, tm, d), lambda i, j: (i, j, 0)),
            pl.BlockSpec((1, d), const),
            pl.BlockSpec((d, hq), const, pipeline_mode=pl.Buffered(1)),
            pl.BlockSpec((hq, d), const, pipeline_mode=pl.Buffered(1)),
            pl.BlockSpec((d, hv), const, pipeline_mode=pl.Buffered(1)),
            pl.BlockSpec((d, hv), const, pipeline_mode=pl.Buffered(1)),
            pl.BlockSpec((tm, half), lambda i, j: (j, 0)),
            pl.BlockSpec((tm, half), lambda i, j: (j, 0)),
            pl.BlockSpec((half, tm), lambda i, j: (0, j)),
            pl.BlockSpec((half, tm), lambda i, j: (0, j)),
        ],
        out_specs=[
            pl.BlockSpec((1, heads, tm, dk), lambda i, j: (i, 0, j, 0)),
            pl.BlockSpec((1, heads, dk, tm), lambda i, j: (i, 0, 0, j)),
            pl.BlockSpec((1, heads, tm, dv), lambda i, j: (i, 0, j, 0)),
            pl.BlockSpec((1, heads, tm, dv), lambda i, j: (i, 0, j, 0)),
        ],
        out_shape=[
            jax.ShapeDtypeStruct((b, heads, s, dk), BF16),
            jax.ShapeDtypeStruct((b, heads, dk, s), BF16),
            jax.ShapeDtypeStruct((b, heads, s, dv), BF16),
            jax.ShapeDtypeStruct((b, heads, s, dv), BF16),
        ],
        compiler_params=_cparams("parallel", "parallel"),
        name="ret_in",
    )(x, g, wq, wkt, wv, wg, cos, sin, cost, sint)


def _retention_kernel(chunk, ld_ref, q_ref, kt_ref, v_ref, sg_ref, o_ref, acc_ref, sf_ref, sb_ref):
    h = pl.program_id(1)
    s = q_ref.shape[2]
    L = chunk
    nc = s // L
    lgf = ld_ref[0, h]
    lgb = ld_ref[1, h]
    ii = lax.broadcasted_iota(I32, (L, L), 0)
    jj = lax.broadcasted_iota(I32, (L, L), 1)
    dist = (ii - jj).astype(F32)
    causal = ii >= jj
    decay = jnp.where(causal,
                      jnp.exp(lgf * jnp.where(causal, dist, 0.0)),
                      jnp.exp(lgb * jnp.where(causal, 0.0, -dist)))
    col = lax.broadcasted_iota(I32, (L, 1), 0).astype(F32)
    row = lax.broadcasted_iota(I32, (1, L), 1).astype(F32)
    q_dec_f = jnp.exp(lgf * (col + 1.0)).astype(BF16)
    q_dec_b = jnp.exp(lgb * (L - col)).astype(BF16)
    k_dec_f = jnp.exp(lgf * (L - 1.0 - row)).astype(BF16)
    k_dec_b = jnp.exp(lgb * row).astype(BF16)
    one = jnp.ones((1, 1), F32)
    c_dec_f = jnp.exp(lgf * L * one)
    c_dec_b = jnp.exp(lgb * L * one)

    def finish(c, out):
        sl = slice(c * L, (c + 1) * L)
        mu = jnp.mean(out, axis=-1, keepdims=True)
        cen = out - mu
        var = jnp.mean(cen * cen, axis=-1, keepdims=True)
        o_ref[0, 0, sl, :] = (sg_ref[0, 0, sl, :].astype(F32) * (cen * lax.rsqrt(var + GN_EPS))
                              ).astype(BF16)

    stored = {}

    def contribute(c, part):
        sl = slice(c * L, (c + 1) * L)
        if c not in stored:
            stored[c] = part is not None
            if part is not None:
                acc_ref[sl, :] = part
        elif not stored[c]:
            finish(c, part)
        else:
            finish(c, acc_ref[sl, :] if part is None else acc_ref[sl, :] + part)

    for k in range(nc):
        cf, cb = k, nc - 1 - k
        slf = slice(cf * L, (cf + 1) * L)
        slb = slice(cb * L, (cb + 1) * L)
        qc, ktc, vc = q_ref[0, 0, slf, :], kt_ref[0, 0, :, slf], v_ref[0, 0, slf, :]
        scores = (_dot(qc, ktc) * decay).astype(BF16)
        part_f = _dot(scores, vc)
        if cf > 0:
            part_f += _dot(qc * q_dec_f, sf_ref[...].astype(BF16))
        if cf < nc - 1:
            upd = _dot(ktc * k_dec_f, vc)
            sf_ref[...] = upd if cf == 0 else sf_ref[...] * c_dec_f + upd
        part_b = None
        if cb < nc - 1:
            part_b = _dot(q_ref[0, 0, slb, :] * q_dec_b, sb_ref[...].astype(BF16))
        if cb > 0:
            upd = _dot(kt_ref[0, 0, :, slb] * k_dec_b, v_ref[0, 0, slb, :])
            sb_ref[...] = upd if cb == nc - 1 else sb_ref[...] * c_dec_b + upd
        contribute(cf, part_f)
        contribute(cb, part_b)


def _retention(q, kt, v, sg, log_decay):
    b, heads, s, dk = q.shape
    dv = v.shape[3]
    chunk = min(RET_CHUNK, s)
    return pl.pallas_call(
        functools.partial(_retention_kernel, chunk),
        grid=(b, heads),
        in_specs=[
            pl.BlockSpec(memory_space=pltpu.SMEM),
            pl.BlockSpec((1, 1, s, dk), lambda i, h: (i, h, 0, 0)),
            pl.BlockSpec((1, 1, dk, s), lambda i, h: (i, h, 0, 0)),
            pl.BlockSpec((1, 1, s, dv), lambda i, h: (i, h, 0, 0)),
            pl.BlockSpec((1, 1, s, dv), lambda i, h: (i, h, 0, 0)),
        ],
        out_specs=pl.BlockSpec((1, 1, s, dv), lambda i, h: (i, h, 0, 0)),
        out_shape=jax.ShapeDtypeStruct((b, heads, s, dv), BF16),
        scratch_shapes=[
            pltpu.VMEM((s, dv), F32),
            pltpu.VMEM((dk, dv), F32),
            pltpu.VMEM((dk, dv), F32),
        ],
        compiler_params=_cparams("parallel", "parallel"),
        name="retention",
    )(log_decay, q, kt, v, sg)


def _ret_out_kernel(o_ref, w_ref, x_ref, gffn_ref, wr_ref, h_ref, hn_ref, l_ref):
    heads, ts = o_ref.shape[1], o_ref.shape[2]
    sub = min(ts, RET_OUT_SUBTILE)
    for r in range(0, ts, sub):
        rows = slice(r, r + sub)
        h = x_ref[0, rows, :]
        for hd in range(heads):
            h = h + _dot(o_ref[0, hd, rows, :], w_ref[hd])
        h_ref[0, rows, :] = h
        _router_epilogue(h, gffn_ref, wr_ref, hn_ref, l_ref, rows)


def _ret_out(o, w_out, x, g_ffn, w_router, l_rows):
    b, heads, s, dv = o.shape
    d = x.shape[2]
    ts = _row_tile(s, 1024)
    tile = lambda i, j: (i, j, 0)
    const = lambda i, j: (0, 0)
    return pl.pallas_call(
        _ret_out_kernel,
        grid=(b, s // ts),
        in_specs=[
            pl.BlockSpec((1, heads, ts, dv), lambda i, j: (i, 0, j, 0)),
            pl.BlockSpec((heads, dv, d), lambda i, j: (0, 0, 0)),
            pl.BlockSpec((1, ts, d), tile),
            pl.BlockSpec((1, d), const),
            pl.BlockSpec(w_router.shape, const),
        ],
        out_specs=[
            pl.BlockSpec((1, ts, d), tile),
            pl.BlockSpec((1, ts, d // 2), tile),
            pl.BlockSpec((1, l_rows, ts), lambda i, j: (i, 0, j)),
        ],
        out_shape=[
            jax.ShapeDtypeStruct((b, s, d), F32),
            jax.ShapeDtypeStruct((b, s, d // 2), I32),
            jax.ShapeDtypeStruct((b, l_rows, s), F32),
        ],
        compiler_params=_cparams("parallel", "parallel"),
        name="ret_out",
    )(o, w_out.reshape(heads, dv, d), x, g_ffn, w_router)


def _exclusive_prefix(flags):
    rows, s = flags.shape
    upper = (lax.broadcasted_iota(I32, (LANES, LANES), 0)
             < lax.broadcasted_iota(I32, (LANES, LANES), 1))
    upper = jnp.where(upper, 1.0, 0.0).astype(BF16)
    carry = jnp.zeros((rows, 1), F32)
    pieces = []
    for k in range(s // LANES):
        blk = flags[:, k * LANES:(k + 1) * LANES]
        pieces.append(_dot(blk.astype(BF16), upper) + carry)
        carry = carry + jnp.sum(blk, axis=1, keepdims=True)
    return jnp.concatenate(pieces, axis=1)


def _router_select_kernel(n_exp, cap, l_ref, rank_ref, gate_ref):
    nb = l_ref.shape[0]
    affs = []
    for i in range(nb):
        l = l_ref[i]
        logits = l[0:n_exp] + l[n_exp:2 * n_exp] + l[2 * n_exp:3 * n_exp]
        e = jnp.exp(logits - jnp.max(logits, axis=0, keepdims=True))
        affs.append(e / jnp.sum(e, axis=0, keepdims=True))
    aff = jnp.concatenate(affs, axis=0)
    keys = pltpu.bitcast(aff, I32)

    def refine(i, thr):
        cand = thr | jnp.left_shift(jnp.int32(1), 30 - i)
        cnt = jnp.sum(jnp.where(keys >= cand, 1.0, 0.0), axis=1, keepdims=True)
        return jnp.where(cnt >= cap, cand, thr)

    thr = lax.fori_loop(0, 31, refine, jnp.zeros((nb * n_exp, 1), I32))
    above = keys > thr
    tied = keys == thr
    n_above = jnp.sum(jnp.where(above, 1.0, 0.0), axis=1, keepdims=True)
    tied_rank = _exclusive_prefix(jnp.where(tied, 1.0, 0.0))
    keep = above | (tied & (tied_rank < cap - n_above))
    rank = _exclusive_prefix(jnp.where(keep, 1.0, 0.0))
    rank = jnp.where(keep, rank.astype(I32), -1)
    gate = jnp.where(keep, aff, 0.0)
    for i in range(nb):
        rank_ref[i] = rank[i * n_exp:(i + 1) * n_exp]
        gate_ref[i] = gate[i * n_exp:(i + 1) * n_exp]


def _router_select(lt, n_exp, cap):
    b, rows, s = lt.shape
    nb = _row_tile(b, 8)
    blk = lambda i: (i, 0, 0)
    return pl.pallas_call(
        functools.partial(_router_select_kernel, n_exp, cap),
        grid=(b // nb,),
        in_specs=[pl.BlockSpec((nb, rows, s), blk)],
        out_specs=[
            pl.BlockSpec((nb, n_exp, s), blk),
            pl.BlockSpec((nb, n_exp, s), blk),
        ],
        out_shape=[
            jax.ShapeDtypeStruct((b, n_exp, s), I32),
            jax.ShapeDtypeStruct((b, n_exp, s), F32),
        ],
        compiler_params=_cparams("parallel"),
        name="router_select",
    )(lt)


def _gather_sc_kernel(n_seq, seq_len, n_exp, e0, ne, cap, win, n_cores, pairs_per_worker,
                      table_hbm, rank_hbm, gate_hbm, zeros_hbm, xe_hbm, gs_hbm,
                      rank_v, gate_v, tok_v, rows_v, gs_v, gsem, osem):
    lanes = SC_LANES
    wid = lax.axis_index("s") * n_cores + lax.axis_index("c")
    lane = lax.broadcasted_iota(I32, (lanes,), 0)
    lane0 = jnp.zeros((lanes,), I32)
    pltpu.sync_copy(zeros_hbm, gs_v)
    n_win = cap // win

    @pl.loop(0, pairs_per_worker)
    def _(i):
        pair = wid * pairs_per_worker + i
        b = pair // ne
        e = pair % ne
        row = b * n_exp + e0 + e
        pltpu.sync_copy(rank_hbm.at[row], rank_v)
        pltpu.sync_copy(gate_hbm.at[row], gate_v)

        @pl.loop(0, seq_len // lanes)
        def _(j):
            r = rank_v[pl.ds(j * lanes, lanes)]
            keep = r >= 0
            slot = jnp.where(keep, r, 0)
            plsc.store_scatter(tok_v, [slot // win, slot % win], lane + (j * lanes + b * seq_len),
                               mask=keep)
            plsc.store_scatter(gs_v, [slot, lane0], gate_v[pl.ds(j * lanes, lanes)], mask=keep)

        out_base = (e * n_seq + b) * cap

        def fetch(w):
            return pltpu.make_async_copy(table_hbm.at[tok_v.at[w]], rows_v.at[w % 2], gsem.at[w % 2])

        def put(w):
            return pltpu.make_async_copy(rows_v.at[w % 2], xe_hbm.at[pl.ds(out_base + w * win, win)],
                                         osem.at[w % 2])

        fetch(0).start()
        for w in range(n_win):
            if w + 1 < n_win:
                if w >= 1:
                    put(w - 1).wait()
                fetch(w + 1).start()
            fetch(w).wait()
            put(w).start()
        if n_win >= 2:
            put(n_win - 2).wait()
        put(n_win - 1).wait()
        pltpu.sync_copy(gs_v, gs_hbm.at[pl.ds(out_base, cap)])


def _gather_sc(table, rank, gate, n_seq, seq_len, n_exp, e0, ne, cap):
    width = table.shape[1]
    info = plsc.get_sparse_core_info()
    n_workers = info.num_cores * info.num_subcores
    pairs = n_seq * ne
    assert info.num_lanes == SC_LANES and pairs % n_workers == 0 and seq_len % SC_LANES == 0
    win = min(cap, SC_GATHER_WINDOW)
    assert cap % win == 0
    mesh = plsc.VectorSubcoreMesh(core_axis_name="c", subcore_axis_name="s")
    call = pl.kernel(
        functools.partial(_gather_sc_kernel, n_seq, seq_len, n_exp, e0, ne, cap, win,
                          info.num_cores, pairs // n_workers),
        mesh=mesh,
        out_type=[jax.ShapeDtypeStruct((ne * n_seq * cap, width), I32),
                  jax.ShapeDtypeStruct((ne * n_seq * cap, LANES), F32)],
        scratch_types=[
            pltpu.VMEM((seq_len,), I32),
            pltpu.VMEM((seq_len,), F32),
            pltpu.VMEM((cap // win, win), I32),
            pltpu.VMEM((2, win, width), I32),
            pltpu.VMEM((cap, LANES), F32),
            pltpu.SemaphoreType.DMA((2,)),
            pltpu.SemaphoreType.DMA((2,)),
        ],
        compiler_params=pltpu.CompilerParams(needs_layout_passes=False),
        name="moe_gather_sc",
    )
    return call(table, rank, gate, jnp.zeros((cap, LANES), F32))


def _ffn_kernel(layer, e0, xe_ref, gs_ref, wg_hbm, wu_hbm, wd_hbm, y_ref,
                wg_s, wu_s, wd_s, stg_g, stg_u, stg_d, sem):
    e = pl.program_id(0)
    m = pl.program_id(1)
    n_exp = pl.num_programs(0)
    nm = pl.num_programs(1)
    n_chunks, d, fc = wg_s.shape[1:]
    cur = e % 2
    par = m % 2

    def chunk_copies(exp, k, buf):
        cols = pl.ds(pl.multiple_of(k * fc, fc), fc)
        return (pltpu.make_async_copy(wg_hbm.at[layer, e0 + exp, :, cols], stg_g.at[buf], sem.at[buf, 0]),
                pltpu.make_async_copy(wu_hbm.at[layer, e0 + exp, :, cols], stg_u.at[buf], sem.at[buf, 1]),
                pltpu.make_async_copy(wd_hbm.at[layer, e0 + exp, cols, :], stg_d.at[buf], sem.at[buf, 2]))

    def convert(buf, slot, k):
        wg_s[slot, k] = stg_g[buf].astype(BF16)
        wu_s[slot, k] = stg_u[buf].astype(BF16)
        wd_s[slot, pl.ds(pl.multiple_of(k * fc, fc), fc), :] = stg_d[buf].astype(BF16)

    first = jnp.logical_and(e == 0, m == 0)

    @pl.when(first)
    def _():
        for c in chunk_copies(0, 0, 0):
            c.start()
        for k in range(n_chunks):
            if k + 1 < n_chunks:
                for c in chunk_copies(0, k + 1, (k + 1) % 2):
                    c.start()
            for c in chunk_copies(0, k, k % 2):
                c.wait()
            if k < n_chunks - 1:
                convert(k % 2, 0, k)

    in_flight = jnp.where(m == 0, e > 0, e < n_exp - 1)

    @pl.when(in_flight)
    def _():
        for c in chunk_copies(0, 0, 1 - par):
            c.wait()

    @pl.when(e < n_exp - 1)
    def _():
        for c in chunk_copies(e + 1, m, par):
            c.start()

    convert(1 - par, jnp.where(m == 0, cur, 1 - cur), jnp.where(m == 0, nm - 1, m - 1))

    x = _unpack_bf16_pairs(xe_ref[0])
    acc = jnp.zeros((x.shape[0], d), F32)
    for k in range(0, n_chunks, 2):
        acts = []
        for kk in (k, k + 1):
            a = _dot(x, wg_s[cur, kk])
            acts.append(a * _sigmoid(a) * _dot(x, wu_s[cur, kk]))
        act = jnp.concatenate(acts, axis=1).astype(BF16)
        acc += _dot(act, wd_s[cur, k * fc:(k + 2) * fc, :])
    y_ref[0] = (acc * gs_ref[0][:, 0:1]).astype(BF16)


def _ffn(layer, e0, xe, gs, wg, wu, wd):
    n_exp, rows, half_d = xe.shape
    d = 2 * half_d
    ff = wg.shape[3]
    fc = min(ff // 2, FFN_WEIGHT_CHUNK)
    nm = ff // fc
    assert ff % fc == 0 and nm % 2 == 0 and rows % nm == 0
    tm = rows // nm
    assert tm % (2 * SUBLANES) == 0
    return pl.pallas_call(
        functools.partial(_ffn_kernel, layer, e0),
        grid=(n_exp, nm),
        in_specs=[
            pl.BlockSpec((1, tm, half_d), lambda e, m: (e, m, 0)),
            pl.BlockSpec((1, tm, LANES), lambda e, m: (e, m, 0)),
            pl.BlockSpec(memory_space=pl.ANY),
            pl.BlockSpec(memory_space=pl.ANY),
            pl.BlockSpec(memory_space=pl.ANY),
        ],
        out_specs=pl.BlockSpec((1, tm, d), lambda e, m: (e, m, 0)),
        out_shape=jax.ShapeDtypeStruct((n_exp, rows, d), BF16),
        scratch_shapes=[
            pltpu.VMEM((2, nm, d, fc), BF16),
            pltpu.VMEM((2, nm, d, fc), BF16),
            pltpu.VMEM((2, ff, d), BF16),
            pltpu.VMEM((2, d, fc), F32),
            pltpu.VMEM((2, d, fc), F32),
            pltpu.VMEM((2, fc, d), F32),
            pltpu.SemaphoreType.DMA((2, 3)),
        ],
        compiler_params=_cparams("arbitrary", "arbitrary"),
        name="moe_ffn",
    )(xe, gs, wg, wu, wd)


def _combine_ple_kernel(final, n_groups, h_ref, *refs):
    y_refs = refs[:n_groups]
    rankt_ref, p_ref, gple_ref, wgate_ref, wproj_ref, gfin_ref, o_ref = refs[n_groups:]
    _, _, cap, d = y_refs[0].shape
    ts = h_ref.shape[1]
    rt = rankt_ref[0]
    slot = lax.broadcasted_iota(I32, (ts, cap), 1)
    h = h_ref[0]
    e0 = 0
    for y_ref in y_refs:
        ne = y_ref.shape[0]
        onehot = jnp.concatenate(
            [jnp.where(rt[:, e:e + 1] == slot, 1.0, 0.0).astype(BF16) for e in range(e0, e0 + ne)],
            axis=1)
        h = h + _dot(onehot, y_ref[:, 0].reshape(ne * cap, d))
        e0 += ne
    gate = _sigmoid(_dot(_rms(h, gple_ref[...]).astype(BF16), wgate_ref[...]))
    h = h + gate * _dot(p_ref[...].astype(BF16), wproj_ref[...])
    if final:
        h = _rms(h, gfin_ref[...])
    o_ref[0] = h


def _combine_ple(layer, h, ys, rankt, p, g_ple, w_gate, w_proj, g_final, final):
    b, s, d = h.shape
    cap = ys[0].shape[2]
    n_exp = sum(y.shape[0] for y in ys)
    ple = p.shape[3]
    ts = _row_tile(s, 1024)
    const = lambda i, j: (0, 0)
    return pl.pallas_call(
        functools.partial(_combine_ple_kernel, final, len(ys)),
        grid=(b, s // ts),
        in_specs=[
            pl.BlockSpec((1, ts, d), lambda i, j: (i, j, 0)),
            *[pl.BlockSpec((y.shape[0], 1, cap, d), lambda i, j: (0, i, 0, 0)) for y in ys],
            pl.BlockSpec((1, ts, n_exp), lambda i, j: (i, j, 0)),
            pl.BlockSpec((None, None, ts, ple), lambda i, j: (layer, i, j, 0)),
            pl.BlockSpec((1, d), const),
            pl.BlockSpec((d, d), const),
            pl.BlockSpec((ple, d), const),
            pl.BlockSpec((1, d), const),
        ],
        out_specs=pl.BlockSpec((1, ts, d), lambda i, j: (i, j, 0)),
        out_shape=jax.ShapeDtypeStruct((b, s, d), F32),
        compiler_params=_cparams("parallel", "arbitrary"),
        name="combine_ple",
    )(h, *ys, rankt, p, g_ple, w_gate, w_proj, g_final)


def _split_router_weight(w):
    d, n_exp = w.shape
    hi = w.astype(BF16)
    r1 = w - hi.astype(F32)
    mid = r1.astype(BF16)
    lo = (r1 - mid.astype(F32)).astype(BF16)
    pad = jnp.zeros((d, LANES - 3 * n_exp), BF16)
    return jnp.concatenate([hi, mid, lo, pad], axis=1)


def _moe_ple(layer, h, hn, lt, p, n_exp, w_gate, w_up, w_down, g_ple, ple_w_gate, ple_w_proj,
             g_final, final):
    b, s, d = h.shape
    cap = EC_CAPACITY_FACTOR * s // n_exp
    rank, gate = _router_select(lt, n_exp, cap)
    split = n_exp - EXPERT_GROUP_SPLIT if 0 < EXPERT_GROUP_SPLIT < n_exp else n_exp
    groups = [(0, split)] + ([(split, n_exp - split)] if split < n_exp else [])
    table = hn.reshape(b * s, d // 2)
    rank2 = rank.reshape(b * n_exp, s)
    gate2 = gate.reshape(b * n_exp, s)
    gathered = [_gather_sc(table, rank2, gate2, b, s, n_exp, e0, ne, cap) for e0, ne in groups]
    ys = []
    for (e0, ne), (xe, gs) in zip(groups, gathered):
        y = _ffn(layer, e0, xe.reshape(ne, b * cap, d // 2), gs.reshape(ne, b * cap, LANES),
                 w_gate, w_up, w_down)
        ys.append(y.reshape(ne, b, cap, d))
    rankt = jnp.swapaxes(rank, 1, 2)
    return _combine_ple(layer, h, ys, rankt, p, g_ple, ple_w_gate, ple_w_proj, g_final, final)


def _rope_tables(s, half):
    inv = ROPE_BASE ** (-jnp.arange(half, dtype=F32) / half)
    ang = jnp.arange(s).astype(F32)[:, None] * inv[None, :]
    return jnp.cos(ang), jnp.sin(ang)


def kernel(x, p, norm_mix, norm_ffn, norm_ple, final_norm, conv_w_in, conv_w, conv_b, conv_w_out,
           ret_w_in, ret_log_decay, ret_w_out, router_w, exp_w_gate, exp_w_up, exp_w_down,
           ple_w_proj, ple_w_gate):
    b, s, d = x.shape
    depth = p.shape[0]
    heads = ret_log_decay.shape[2]
    n_exp = router_w.shape[2]
    hq = d
    hv = (ret_w_in.shape[2] - 2 * hq) // 2
    h = x
    for i in range(depth):
        j = i // 2
        g_mix = norm_mix[i][None, :]
        g_ffn = norm_ffn[i][None, :]
        w_router = _split_router_weight(router_w[i])
        if i % 2 == 0:
            h, hn, lt = _conv_mixer(h, g_mix, conv_w_in[j].astype(BF16), conv_w[j],
                                      conv_b[j][None, :], conv_w_out[j].astype(BF16),
                                      g_ffn, w_router, 3 * n_exp)
        else:
            w_in = ret_w_in[j]
            wq = w_in[:, :hq].astype(BF16)
            wkt = w_in[:, hq:2 * hq].T.astype(BF16)
            wv = w_in[:, 2 * hq:2 * hq + hv].astype(BF16)
            wg = w_in[:, 2 * hq + hv:].astype(BF16)
            cos, sin = _rope_tables(s, hq // heads // 2)
            q, kt, v, sg = _ret_in(h, g_mix, wq, wkt, wv, wg, cos, sin, heads)
            o = _retention(q, kt, v, sg, ret_log_decay[j])
            h, hn, lt = _ret_out(o, ret_w_out[j].astype(BF16), h, g_ffn, w_router, 3 * n_exp)
        h = _moe_ple(i, h, hn, lt, p, n_exp, exp_w_gate, exp_w_up, exp_w_down,
                     norm_ple[i][None, :],
                     ple_w_gate[i].astype(BF16), ple_w_proj[i].astype(BF16),
                     final_norm[None, :], i == depth - 1)
    return h
```
